```python
import math
import jax, jax.numpy as jnp
from jax import lax
import numpy as np

D_MODEL = 4096
BATCH = 1
SEQ = 8192
DEPTH = 2
DEC_BATCH = 16
DEC_SEQ = 16
PAST_LEN = 4096

CHUNK = 64
NORM_EPS = 1e-5
A_HEADS = 16
A_KV_HEADS = 2
HEAD_DIM = 64
A_WIDTH = A_HEADS * HEAD_DIM
A_KV_WIDTH = A_KV_HEADS * HEAD_DIM
WINDOW = 128
WINDOW_CHUNKS = WINDOW // CHUNK
ROPE_THETA = 10000.0
ATTN_SCALE = HEAD_DIM ** -0.5
RW_HEAD = 64
RW_HEADS = 24
RW_WIDTH = RW_HEADS * RW_HEAD
RW_LORA = 64
RW_LN_EPS = 64e-5
SHIFT_WIDTH = 3 * RW_WIDTH + 3 * RW_LORA
RW_SPLITS = (RW_WIDTH, 2 * RW_WIDTH, 3 * RW_WIDTH, 3 * RW_WIDTH + RW_LORA, 3 * RW_WIDTH + 2 * RW_LORA)
S5_GROUPS = 96
S5_GROUP = 16
S5_WIDTH = S5_GROUPS * S5_GROUP
S5_STATE = 64
OFF_Q = A_WIDTH
OFF_K = OFF_Q + A_KV_WIDTH
OFF_V = OFF_K + A_KV_WIDTH
OFF_RW = OFF_V + SHIFT_WIDTH
OFF_S5 = OFF_RW + S5_WIDTH
OFF_GA = OFF_S5 + D_MODEL
OFF_GB = OFF_GA + D_MODEL
IN_COLS = OFF_GB + D_MODEL
SPLIT_POINTS = (OFF_Q, OFF_K, OFF_V, OFF_RW, OFF_S5, OFF_GA, OFF_GB)
N_GROUPS = 4
EXPERTS_PER_GROUP = 8
N_EXPERTS = N_GROUPS * EXPERTS_PER_GROUP
TOP_K = 2
D_EXPERT = 1024

kernel_name = 'hybrid_streaming_encoder_step'


def rms_norm(x, g):
    xf = x.astype(jnp.float32)
    y = xf * lax.rsqrt(jnp.mean(xf * xf, axis=-1, keepdims=True) + NORM_EPS)
    return (y * g.astype(jnp.float32)).astype(x.dtype)


def rope(x, pos):
    half = HEAD_DIM // 2
    inv_freq = ROPE_THETA ** (-jnp.arange(half, dtype=jnp.float32) / half)
    ang = pos.astype(jnp.float32)[:, None] * inv_freq[None, :]
    cos = jnp.cos(ang)[:, None, :]
    sin = jnp.sin(ang)[:, None, :]
    xf = x.astype(jnp.float32)
    x1, x2 = xf[..., :half], xf[..., half:]
    return jnp.concatenate([x1 * cos - x2 * sin, x2 * cos + x1 * sin], axis=-1).astype(x.dtype)


def softmax_with_sink(s, sink, mask):
    s = jnp.where(mask, s, -jnp.inf)
    m = jnp.maximum(jnp.max(s, axis=-1, keepdims=True), sink)
    p = jnp.exp(s - m)
    return p / (jnp.sum(p, axis=-1, keepdims=True) + jnp.exp(sink - m))


def attn_banded(q, k, v, sinks):
    B, T = q.shape[:2]
    nc = T // CHUNK
    grp = A_HEADS // A_KV_HEADS
    qc = q.reshape(B, nc, CHUNK, A_KV_HEADS, grp, HEAD_DIM)

    def band(t):
        t = t.reshape(B, nc, CHUNK, A_KV_HEADS, HEAD_DIM)
        tp = jnp.pad(t, ((0, 0), (WINDOW_CHUNKS, 0), (0, 0), (0, 0), (0, 0)))
        return jnp.concatenate([tp[:, o:o + nc] for o in range(WINDOW_CHUNKS + 1)], axis=2)

    kb, vb = band(k), band(v)
    key_chunk = jnp.arange(nc)[:, None] + jnp.repeat(jnp.arange(-WINDOW_CHUNKS, 1), CHUNK)[None, :]
    mask = (key_chunk >= 0)[None, :, None, None, None, :]
    s = jnp.einsum('bcqhgd,bckhd->bchgqk', qc, kb, preferred_element_type=jnp.float32) * ATTN_SCALE
    sink = sinks.astype(jnp.float32).reshape(1, 1, A_KV_HEADS, grp, 1, 1)
    p = softmax_with_sink(s, sink, mask)
    o = jnp.einsum('bchgqk,bckhd->bcqhgd', p.astype(v.dtype), vb)
    return o.reshape(B, T, A_WIDTH)


def attn_cached(q, k_all, v_all, sinks):
    B, S = q.shape[:2]
    grp = A_HEADS // A_KV_HEADS
    qg = q.reshape(B, S, A_KV_HEADS, grp, HEAD_DIM)
    s = jnp.einsum('bqhgd,bkhd->bhgqk', qg, k_all, preferred_element_type=jnp.float32) * ATTN_SCALE
    sink = sinks.astype(jnp.float32).reshape(1, A_KV_HEADS, grp, 1, 1)
    p = softmax_with_sink(s, sink, jnp.ones(s.shape[-1:], dtype=bool))
    o = jnp.einsum('bhgqk,bkhd->bqhgd', p.astype(v_all.dtype), v_all)
    return o.reshape(B, S, A_WIDTH)


def rwkv7_branch(f, shift_prev, state0, mu, w0, w2, a0, a2, g2, k_k, k_a, r_k, ln_w, ln_b):
    B, T, _ = f.shape
    ff = f.astype(jnp.float32)
    f_prev = jnp.concatenate([shift_prev.astype(jnp.float32)[:, None, :], ff[:, :-1]], axis=1)
    fs = ff + (f_prev - ff) * mu
    r, k, v, xw, xa, xg = jnp.split(fs, RW_SPLITS, axis=-1)
    w_log = -jax.nn.softplus(-(w0 + jnp.tanh(xw) @ w2)) - 0.5
    decay = jnp.exp(-jnp.exp(w_log))
    a = jax.nn.sigmoid(a0 + xa @ a2)
    g = jax.nn.sigmoid(xg) @ g2
    hs = (B, T, RW_HEADS, RW_HEAD)
    r, k, v, decay, a = (t.reshape(hs) for t in (r, k, v, decay, a))
    kk = k * k_k.reshape(RW_HEADS, RW_HEAD)
    kk = kk / jnp.maximum(jnp.sqrt(jnp.sum(kk * kk, axis=-1, keepdims=True)), 1e-12)
    k = k * (1.0 + (a - 1.0) * k_a.reshape(RW_HEADS, RW_HEAD))

    def step(S, inp):
        r_t, w_t, k_t, v_t, kk_t, a_t = inp
        s_kk = jnp.einsum('bhvk,bhk->bhv', S, -kk_t)
        S = (S * w_t[:, :, None, :] + s_kk[..., None] * (kk_t * a_t)[:, :, None, :]
             + v_t[..., None] * k_t[:, :, None, :])
        return S, jnp.einsum('bhvk,bhk->bhv', S, r_t)

    xs = tuple(jnp.moveaxis(t, 1, 0) for t in (r, decay, k, v, kk, a))
    state, ys = lax.scan(step, state0.astype(jnp.float32), xs)
    y = jnp.moveaxis(ys, 0, 1)
    mean = jnp.mean(y, axis=-1, keepdims=True)
    var = jnp.mean(jnp.square(y - mean), axis=-1, keepdims=True)
    y = (y - mean) * lax.rsqrt(var + RW_LN_EPS) * ln_w.reshape(RW_HEADS, RW_HEAD) + ln_b.reshape(RW_HEADS, RW_HEAD)
    bonus = jnp.sum(r * k * r_k, axis=-1, keepdims=True) * v
    out = (y + bonus).reshape(B, T, RW_WIDTH) * g
    return out.astype(f.dtype), state, f[:, -1]


def s5_branch(u, x0_re, x0_im, lam_re, lam_im, b_re, b_im, c_re, c_im, d, log_step, glu_w, glu_b):
    B, T, _ = u.shape
    f32 = jnp.float32
    uf = u.astype(f32).reshape(B, T, S5_GROUPS, S5_GROUP)
    lam = lax.complex(lam_re.astype(f32), lam_im.astype(f32))
    dt = jnp.exp(log_step.astype(f32))[:, None]
    lam_bar = jnp.exp(lam * dt)
    b_bar = ((lam_bar - 1.0) / lam)[:, :, None] * lax.complex(b_re.astype(f32), b_im.astype(f32))
    bu = jnp.einsum('gph,btgh->btgp', b_bar, uf.astype(jnp.complex64))
    bu = bu.at[:, 0].add(lam_bar * lax.complex(x0_re.astype(f32), x0_im.astype(f32)))

    def combine(e1, e2):
        a1, b1 = e1
        a2, b2 = e2
        return a1 * a2, a2 * b1 + b2

    _, xs = lax.associative_scan(combine, (jnp.broadcast_to(lam_bar, bu.shape), bu), axis=1)
    c = lax.complex(c_re.astype(f32), c_im.astype(f32))
    y = jnp.einsum('ghp,btgp->btgh', c, xs).real + d.astype(f32).reshape(S5_GROUPS, S5_GROUP) * uf
    z = jax.nn.gelu(y.reshape(B, T, S5_WIDTH))
    out = z * jax.nn.sigmoid(z @ glu_w.astype(f32) + glu_b.astype(f32))
    x_last = xs[:, -1]
    return out.astype(u.dtype), jnp.real(x_last), jnp.imag(x_last)


def routed_experts(xt, expert_id, gate, w_gate, w_up, w_down):
    n, D = xt.shape
    a = expert_id.shape[0]
    blk = max(8, min(256, a // N_EXPERTS))
    n_blocks = -(-(a + N_EXPERTS * (blk - 1)) // blk)
    idx = jnp.arange(a, dtype=jnp.int32)
    tok = idx // TOP_K
    order = jnp.argsort(expert_id * a + idx)
    e_sorted = expert_id[order]
    counts = jnp.bincount(expert_id, length=N_EXPERTS)
    starts = jnp.cumsum(counts) - counts
    padded = (counts + blk - 1) // blk * blk
    pad_end = jnp.cumsum(padded)
    pad_start = pad_end - padded
    slot_sorted = (pad_start[e_sorted] + idx - starts[e_sorted]).astype(jnp.int32)
    slot = jnp.zeros((a,), jnp.int32).at[order].set(slot_sorted)
    slot_tok = jnp.full((n_blocks * blk,), n, jnp.int32).at[slot].set(tok)
    blk_expert = jnp.minimum(
        jnp.searchsorted(pad_end, jnp.arange(n_blocks, dtype=jnp.int32) * blk, side='right'), N_EXPERTS - 1)
    x_pad = jnp.concatenate([xt, jnp.zeros((1, D), xt.dtype)], axis=0)

    def expert_block(args):
        toks, e = args
        xb = x_pad[toks]
        h = jax.nn.silu(xb @ w_gate[e]) * (xb @ w_up[e])
        return h @ w_down[e]

    out = lax.map(expert_block, (slot_tok.reshape(n_blocks, blk), blk_expert)).reshape(n_blocks * blk, D)
    contrib = out[slot] * gate[:, None].astype(out.dtype)
    return jnp.sum(contrib.reshape(n, TOP_K, D), axis=1)


def hier_moe(x, rg_w, rg_b, re_w, re_b, w_gate, w_up, w_down):
    B, T, D = x.shape
    n = B * T
    xt = x.reshape(n, D)
    g_logits = jnp.einsum('nd,dg->ng', xt, rg_w, preferred_element_type=jnp.float32) + rg_b.astype(jnp.float32)
    grp = jnp.argmax(g_logits, axis=-1).astype(jnp.int32)
    p_grp = jnp.take_along_axis(jax.nn.softmax(g_logits, axis=-1), grp[:, None], axis=-1)
    e_logits = jnp.einsum('nd,de->ne', xt, re_w, preferred_element_type=jnp.float32) + re_b.astype(jnp.float32)
    e_logits = e_logits.reshape(n, N_GROUPS, EXPERTS_PER_GROUP)
    e_logits = jnp.take_along_axis(e_logits, grp[:, None, None], axis=1)[:, 0]
    top_v, top_i = lax.top_k(e_logits, TOP_K)
    gate = jax.nn.softmax(top_v, axis=-1) * p_grp
    expert_id = (grp[:, None] * EXPERTS_PER_GROUP + top_i).astype(jnp.int32)
    y = routed_experts(xt, expert_id.reshape(-1), gate.reshape(-1), w_gate, w_up, w_down)
    return y.reshape(B, T, D)


def mixing_sublayer(x, pos, k_hist, v_hist, shift_prev, rw_state, s5_re, s5_im, lw):
    B, T, _ = x.shape
    xn = rms_norm(x, lw['norm_mix_g'])
    comb = xn @ lw['w_in']
    q, k, v, f_rw, u_s5, g_a, g_b, g_c = jnp.split(comb, SPLIT_POINTS, axis=-1)
    q = rope(q.reshape(B, T, A_HEADS, HEAD_DIM), pos)
    k = rope(k.reshape(B, T, A_KV_HEADS, HEAD_DIM), pos)
    v = v.reshape(B, T, A_KV_HEADS, HEAD_DIM)
    if k_hist is None:
        o_a = attn_banded(q, k, v, lw['attn_sinks'])
        k_all, v_all = k, v
    else:
        k_all = jnp.concatenate([k_hist.astype(k.dtype), k], axis=1)
        v_all = jnp.concatenate([v_hist.astype(v.dtype), v], axis=1)
        o_a = attn_cached(q, k_all, v_all, lw['attn_sinks'])
    o_b, rw_new, shift_new = rwkv7_branch(
        f_rw, shift_prev, rw_state, lw['rwkv_mu'], lw['rwkv_w0'], lw['rwkv_w2'], lw['rwkv_a0'], lw['rwkv_a2'],
        lw['rwkv_g2'], lw['rwkv_k_k'], lw['rwkv_k_a'], lw['rwkv_r_k'], lw['rwkv_ln_w'], lw['rwkv_ln_b'])
    o_c, s5_re_new, s5_im_new = s5_branch(
        u_s5, s5_re, s5_im, lw['s5_lambda_re'], lw['s5_lambda_im'], lw['s5_b_re'], lw['s5_b_im'],
        lw['s5_c_re'], lw['s5_c_im'], lw['s5_d'], lw['s5_log_step'], lw['s5_glu_w'], lw['s5_glu_b'])
    merged = (jax.nn.sigmoid(g_a) * (o_a @ lw['w_branch_attn'])
              + jax.nn.sigmoid(g_b) * (o_b @ lw['w_branch_rwkv'])
              + jax.nn.sigmoid(g_c) * (o_c @ lw['w_branch_s5']))
    x = x + merged @ lw['w_out']
    return x, (k_all[:, -WINDOW:], v_all[:, -WINDOW:], shift_new, rw_new, s5_re_new, s5_im_new)


def ffn_sublayer(x, lw):
    return x + hier_moe(rms_norm(x, lw['norm_ffn_g']), lw['router_group_w'], lw['router_group_b'],
                        lw['router_expert_w'], lw['router_expert_b'], lw['expert_w_gate'],
                        lw['expert_w_up'], lw['expert_w_down'])


def setup_inputs(seed: int = 0) -> dict:
    key = jax.random.key(seed)
    ks = iter(jax.random.split(key, 64))

    def nrm(shape, scale):
        return jax.random.normal(next(ks), shape, jnp.float32) * scale

    def uni(shape, lo, hi):
        return jax.random.uniform(next(ks), shape, jnp.float32, lo, hi)

    L = DEPTH
    lam_im_base = jnp.pi * jnp.arange(S5_STATE, dtype=jnp.float32)
    return {
        'x_prompt': nrm((BATCH, SEQ, D_MODEL), 1.0),
        'x_sample': nrm((DEC_BATCH, DEC_SEQ, D_MODEL), 1.0),
        'cache_k': nrm((L, DEC_BATCH, WINDOW, A_KV_HEADS, HEAD_DIM), 1.0),
        'cache_v': nrm((L, DEC_BATCH, WINDOW, A_KV_HEADS, HEAD_DIM), 1.0),
        'state_shift': nrm((L, DEC_BATCH, SHIFT_WIDTH), 1.0),
        'state_rwkv': nrm((L, DEC_BATCH, RW_HEADS, RW_HEAD, RW_HEAD), 0.5),
        'state_s5_re': nrm((L, DEC_BATCH, S5_GROUPS, S5_STATE), 0.5),
        'state_s5_im': nrm((L, DEC_BATCH, S5_GROUPS, S5_STATE), 0.5),
        'norm_mix_g': 1.0 + nrm((L, D_MODEL), 0.02),
        'w_in': nrm((L, D_MODEL, IN_COLS), D_MODEL ** -0.5),
        'attn_sinks': nrm((L, A_HEADS), 0.5),
        'rwkv_mu': uni((L, SHIFT_WIDTH), 0.0, 1.0),
        'rwkv_w0': uni((L, RW_WIDTH), -6.0, -1.0),
        'rwkv_w2': nrm((L, RW_LORA, RW_WIDTH), 0.5 * RW_LORA ** -0.5),
        'rwkv_a0': nrm((L, RW_WIDTH), 0.5),
        'rwkv_a2': nrm((L, RW_LORA, RW_WIDTH), 0.5 * RW_LORA ** -0.5),
        'rwkv_g2': nrm((L, RW_LORA, RW_WIDTH), RW_LORA ** -0.5),
        'rwkv_k_k': 0.85 + nrm((L, RW_WIDTH), 0.05),
        'rwkv_k_a': 1.0 + nrm((L, RW_WIDTH), 0.05),
        'rwkv_r_k': nrm((L, RW_HEADS, RW_HEAD), 0.1),
        'rwkv_ln_w': 1.0 + nrm((L, RW_WIDTH), 0.02),
        'rwkv_ln_b': nrm((L, RW_WIDTH), 0.02),
        's5_lambda_re': -0.5 + nrm((L, S5_GROUPS, S5_STATE), 0.01),
        's5_lambda_im': lam_im_base + nrm((L, S5_GROUPS, S5_STATE), 0.01),
        's5_b_re': nrm((L, S5_GROUPS, S5_STATE, S5_GROUP), (2 * S5_GROUP) ** -0.5),
        's5_b_im': nrm((L, S5_GROUPS, S5_STATE, S5_GROUP), (2 * S5_GROUP) ** -0.5),
        's5_c_re': nrm((L, S5_GROUPS, S5_GROUP, S5_STATE), (2 * S5_STATE) ** -0.5),
        's5_c_im': nrm((L, S5_GROUPS, S5_GROUP, S5_STATE), (2 * S5_STATE) ** -0.5),
        's5_d': nrm((L, S5_WIDTH), 1.0),
        's5_log_step': uni((L, S5_GROUPS), math.log(1e-3), math.log(1e-1)),
        's5_glu_w': nrm((L, S5_WIDTH, S5_WIDTH), S5_WIDTH ** -0.5),
        's5_glu_b': nrm((L, S5_WIDTH), 0.02),
        'w_branch_attn': nrm((L, A_WIDTH, D_MODEL), A_WIDTH ** -0.5),
        'w_branch_rwkv': nrm((L, RW_WIDTH, D_MODEL), RW_WIDTH ** -0.5),
        'w_branch_s5': nrm((L, S5_WIDTH, D_MODEL), S5_WIDTH ** -0.5),
        'w_out': nrm((L, D_MODEL, D_MODEL), D_MODEL ** -0.5),
        'norm_ffn_g': 1.0 + nrm((L, D_MODEL), 0.02),
        'router_group_w': nrm((L, D_MODEL, N_GROUPS), D_MODEL ** -0.5),
        'router_group_b': nrm((L, N_GROUPS), 0.01),
        'router_expert_w': nrm((L, D_MODEL, N_EXPERTS), D_MODEL ** -0.5),
        'router_expert_b': nrm((L, N_EXPERTS), 0.01),
        'expert_w_gate': nrm((L, N_EXPERTS, D_MODEL, D_EXPERT), D_MODEL ** -0.5),
        'expert_w_up': nrm((L, N_EXPERTS, D_MODEL, D_EXPERT), D_MODEL ** -0.5),
        'expert_w_down': nrm((L, N_EXPERTS, D_EXPERT, D_MODEL), D_EXPERT ** -0.5),
        'norm_final_g': 1.0 + nrm((D_MODEL,), 0.02),
    }


def reference(x_prompt, x_sample, cache_k, cache_v, state_shift, state_rwkv, state_s5_re, state_s5_im,
              norm_mix_g, w_in, attn_sinks, rwkv_mu, rwkv_w0, rwkv_w2, rwkv_a0, rwkv_a2, rwkv_g2,
              rwkv_k_k, rwkv_k_a, rwkv_r_k, rwkv_ln_w, rwkv_ln_b, s5_lambda_re, s5_lambda_im,
              s5_b_re, s5_b_im, s5_c_re, s5_c_im, s5_d, s5_log_step, s5_glu_w, s5_glu_b,
              w_branch_attn, w_branch_rwkv, w_branch_s5, w_out, norm_ffn_g, router_group_w,
              router_group_b, router_expert_w, router_expert_b, expert_w_gate, expert_w_up,
              expert_w_down, norm_final_g):
    B = x_prompt.shape[0]
    pos_p = jnp.arange(x_prompt.shape[1], dtype=jnp.int32)
    pos_s = PAST_LEN + jnp.arange(x_sample.shape[1], dtype=jnp.int32)
    zero_shift = jnp.zeros((B, SHIFT_WIDTH), x_prompt.dtype)
    zero_rw = jnp.zeros((B, RW_HEADS, RW_HEAD, RW_HEAD), jnp.float32)
    zero_s5 = jnp.zeros((B, S5_GROUPS, S5_STATE), jnp.float32)
    hp, hs = x_prompt, x_sample
    new_p = [[] for _ in range(6)]
    new_s = [[] for _ in range(6)]
    for l in range(DEPTH):
        lw = dict(
            norm_mix_g=norm_mix_g[l], w_in=w_in[l], attn_sinks=attn_sinks[l], rwkv_mu=rwkv_mu[l],
            rwkv_w0=rwkv_w0[l], rwkv_w2=rwkv_w2[l], rwkv_a0=rwkv_a0[l], rwkv_a2=rwkv_a2[l],
            rwkv_g2=rwkv_g2[l], rwkv_k_k=rwkv_k_k[l], rwkv_k_a=rwkv_k_a[l], rwkv_r_k=rwkv_r_k[l],
            rwkv_ln_w=rwkv_ln_w[l], rwkv_ln_b=rwkv_ln_b[l], s5_lambda_re=s5_lambda_re[l],
            s5_lambda_im=s5_lambda_im[l], s5_b_re=s5_b_re[l], s5_b_im=s5_b_im[l], s5_c_re=s5_c_re[l],
            s5_c_im=s5_c_im[l], s5_d=s5_d[l], s5_log_step=s5_log_step[l], s5_glu_w=s5_glu_w[l],
            s5_glu_b=s5_glu_b[l], w_branch_attn=w_branch_attn[l], w_branch_rwkv=w_branch_rwkv[l],
            w_branch_s5=w_branch_s5[l], w_out=w_out[l], norm_ffn_g=norm_ffn_g[l],
            router_group_w=router_group_w[l], router_group_b=router_group_b[l],
            router_expert_w=router_expert_w[l], router_expert_b=router_expert_b[l],
            expert_w_gate=expert_w_gate[l], expert_w_up=expert_w_up[l], expert_w_down=expert_w_down[l])
        hp, st_p = mixing_sublayer(hp, pos_p, None, None, zero_shift, zero_rw, zero_s5, zero_s5, lw)
        hp = ffn_sublayer(hp, lw)
        hs, st_s = mixing_sublayer(hs, pos_s, cache_k[l], cache_v[l], state_shift[l], state_rwkv[l],
                                   state_s5_re[l], state_s5_im[l], lw)
        hs = ffn_sublayer(hs, lw)
        for i in range(6):
            new_p[i].append(st_p[i])
            new_s[i].append(st_s[i])
    y_prompt = rms_norm(hp, norm_final_g)
    y_sample = rms_norm(hs, norm_final_g)
    k_p, v_p, shift_p, rw_p, s5re_p, s5im_p = (jnp.stack(t, axis=0) for t in new_p)
    k_s, v_s, shift_s, rw_s, s5re_s, s5im_s = (jnp.stack(t, axis=0) for t in new_s)
    return (y_prompt, y_sample, k_p, v_p, shift_p, rw_p, s5re_p, s5im_p,
            k_s, v_s, shift_s, rw_s, s5re_s, s5im_s)
```

```python
import functools
import math

import jax
import jax.numpy as jnp
from jax import lax
from jax.experimental import pallas as pl
from jax.experimental.pallas import tpu as pltpu

F32 = jnp.float32
BF16 = jnp.bfloat16
HIGHEST = lax.Precision.HIGHEST

D_MODEL = 4096
CHUNK = 64
NORM_EPS = 1e-5
A_HEADS = 16
A_KV_HEADS = 2
HEAD_DIM = 64
A_WIDTH = A_HEADS * HEAD_DIM
A_KV_WIDTH = A_KV_HEADS * HEAD_DIM
WINDOW = 128
ROPE_THETA = 10000.0
ATTN_SCALE = HEAD_DIM ** -0.5
PAST_LEN = 4096
RW_HEAD = 64
RW_HEADS = 24
RW_WIDTH = RW_HEADS * RW_HEAD
RW_LORA = 64
RW_LN_EPS = 64e-5
SHIFT_WIDTH = 3 * RW_WIDTH + 3 * RW_LORA
S5_GROUPS = 96
S5_GROUP = 16
S5_WIDTH = S5_GROUPS * S5_GROUP
S5_STATE = 64
OFF_Q = A_WIDTH
OFF_K = OFF_Q + A_KV_WIDTH
OFF_V = OFF_K + A_KV_WIDTH
OFF_RW = OFF_V + SHIFT_WIDTH
OFF_S5 = OFF_RW + S5_WIDTH
OFF_GA = OFF_S5 + D_MODEL
OFF_GB = OFF_GA + D_MODEL
IN_COLS = OFF_GB + D_MODEL
N_GROUPS = 4
EXPERTS_PER_GROUP = 8
N_EXPERTS = N_GROUPS * EXPERTS_PER_GROUP
TOP_K = 2
D_EXPERT = 1024

LANES = 128
SUBLANES = 8
RW_PAD = 4864
RW_LORA_PAD = RW_PAD - 3 * RW_WIDTH
S5_TILE_GROUPS = 8
S5_TILE_IN = S5_TILE_GROUPS * S5_GROUP
S5_TILE_ST = S5_TILE_GROUPS * S5_STATE
S5_TILES = S5_GROUPS // S5_TILE_GROUPS
MOE_TM = 320
MOE_TF = 256


def _cp(sem, vmem_mb=48):
    return pltpu.CompilerParams(dimension_semantics=sem, vmem_limit_bytes=vmem_mb * 2 ** 20)


def _pick(n, prefs):
    for p in prefs:
        if n % p == 0:
            return p
    raise ValueError(f"no tile for {n} in {prefs}")


def _rmsnorm_body(x_ref, g_ref, o_ref):
    x = x_ref[...]
    inv = lax.rsqrt(jnp.mean(x * x, axis=-1, keepdims=True) + NORM_EPS)
    o_ref[...] = (x * inv * g_ref[...]).astype(o_ref.dtype)


def _rmsnorm(x, g, out_dtype):
    m, d = x.shape
    tm = _pick(m, (256, 128, 16, 8))
    return pl.pallas_call(
        _rmsnorm_body, grid=(m // tm,),
        in_specs=[pl.BlockSpec((tm, d), lambda i: (i, 0)), pl.BlockSpec((1, d), lambda i: (0, 0))],
        out_specs=pl.BlockSpec((tm, d), lambda i: (i, 0)),
        out_shape=jax.ShapeDtypeStruct((m, d), out_dtype),
        compiler_params=_cp(("parallel",)), name="rmsnorm")(x, g.reshape(1, d))


def _mm_body(a_ref, b_ref, o_ref, *, act):
    acc = jnp.dot(a_ref[...], b_ref[...], preferred_element_type=F32)
    if act == "sigmoid":
        acc = jax.nn.sigmoid(acc)
    o_ref[...] = acc.astype(o_ref.dtype)


def _mm_res_body(a_ref, b_ref, r_ref, o_ref):
    o_ref[...] = r_ref[...] + jnp.dot(a_ref[...], b_ref[...], preferred_element_type=F32)


def _row_tile(m):
    return _pick(m, (768, 512, 256, 128, 64, 16, 8))


def _matmul(a, b, *, col0=0, n=None, act=None, out_dtype=F32, tn=512, name="matmul"):
    m, k = a.shape
    n = b.shape[1] if n is None else n
    tm = _row_tile(m)
    tn = _pick(math.gcd(n, col0) if col0 else n, (tn, 256, 128))
    j0 = col0 // tn
    return pl.pallas_call(
        functools.partial(_mm_body, act=act), grid=(m // tm, n // tn),
        in_specs=[pl.BlockSpec((tm, k), lambda i, j: (i, 0)), pl.BlockSpec((k, tn), lambda i, j: (0, j0 + j))],
        out_specs=pl.BlockSpec((tm, tn), lambda i, j: (i, j)),
        out_shape=jax.ShapeDtypeStruct((m, n), out_dtype),
        compiler_params=_cp(("parallel", "parallel")), name=name)(a, b)


def _matmul_residual(a, b, res, *, tn=512):
    m, k = a.shape
    n = b.shape[1]
    tm = _row_tile(m)
    return pl.pallas_call(
        _mm_res_body, grid=(m // tm, n // tn),
        in_specs=[pl.BlockSpec((tm, k), lambda i, j: (i, 0)), pl.BlockSpec((k, tn), lambda i, j: (0, j)),
                  pl.BlockSpec((tm, tn), lambda i, j: (i, j))],
        out_specs=pl.BlockSpec((tm, tn), lambda i, j: (i, j)),
        out_shape=jax.ShapeDtypeStruct((m, n), F32),
        compiler_params=_cp(("parallel", "parallel")), name="matmul_residual")(a, b, res)


def _rope_body(x_ref, cos_ref, sin_ref, q_ref, k_ref):
    cos = cos_ref[...]
    sin = sin_ref[...]
    lane = lax.broadcasted_iota(jnp.int32, cos.shape, 1)
    first = (lane % HEAD_DIM) < (HEAD_DIM // 2)
    for j in range((A_WIDTH + A_KV_WIDTH) // LANES):
        x = x_ref[:, j * LANES:(j + 1) * LANES]
        partner = jnp.where(first, pltpu.roll(x, LANES - HEAD_DIM // 2, 1), pltpu.roll(x, HEAD_DIM // 2, 1))
        y = x * cos + partner * sin
        if j < A_WIDTH // LANES:
            q_ref[:, j * LANES:(j + 1) * LANES] = y
        else:
            k_ref[...] = y


def _rope(qkv, cos, sin):
    m, w = qkv.shape
    tm = _pick(m, (256, 128, 16, 8))
    return pl.pallas_call(
        _rope_body, grid=(m // tm,),
        in_specs=[pl.BlockSpec((tm, w), lambda i: (i, 0)), pl.BlockSpec((tm, LANES), lambda i: (i, 0)),
                  pl.BlockSpec((tm, LANES), lambda i: (i, 0))],
        out_specs=[pl.BlockSpec((tm, A_WIDTH), lambda i: (i, 0)), pl.BlockSpec((tm, LANES), lambda i: (i, 0))],
        out_shape=[jax.ShapeDtypeStruct((m, A_WIDTH), F32), jax.ShapeDtypeStruct((m, LANES), F32)],
        compiler_params=_cp(("parallel",)), name="rope")(qkv, cos, sin)


def _attend(q, k_all, v_all, sink_ref, valid, o_ref):
    rows = q.shape[0]
    grp = A_HEADS // A_KV_HEADS
    outs = []
    for h in range(A_KV_HEADS):
        kh = k_all[:, h * HEAD_DIM:(h + 1) * HEAD_DIM]
        vh = v_all[:, h * HEAD_DIM:(h + 1) * HEAD_DIM]
        qs = jnp.concatenate(
            [q[:, (h * grp + g) * HEAD_DIM:(h * grp + g + 1) * HEAD_DIM] for g in range(grp)], axis=0).astype(BF16)
        s = lax.dot_general(qs, kh, (((1,), (1,)), ((), ())), preferred_element_type=F32) * ATTN_SCALE
        if valid is not None:
            s = jnp.where(valid, s, -jnp.inf)
        sink = jnp.concatenate([jnp.full((rows, 1), sink_ref[h * grp + g], F32) for g in range(grp)], axis=0)
        m = jnp.maximum(jnp.max(s, axis=-1, keepdims=True), sink)
        p = jnp.exp(s - m)
        den = jnp.sum(p, axis=-1, keepdims=True) + jnp.exp(sink - m)
        o = jnp.dot((p / den).astype(BF16), vh, preferred_element_type=F32)
        outs += [o[g * rows:(g + 1) * rows] for g in range(grp)]
    o_ref[...] = jnp.concatenate(outs, axis=1).astype(o_ref.dtype)


def _attn_prompt_body(sink_ref, q_ref, k0_ref, k1_ref, k2_ref, v0_ref, v1_ref, v2_ref, o_ref):
    i = pl.program_id(0)
    k_all = jnp.concatenate([k0_ref[...], k1_ref[...], k2_ref[...]], axis=0).astype(BF16)
    v_all = jnp.concatenate([v0_ref[...], v1_ref[...], v2_ref[...]], axis=0).astype(BF16)
    col = lax.broadcasted_iota(jnp.int32, (1, 3 * CHUNK), 1)
    valid = (col >= 2 * CHUNK) | ((col >= CHUNK) & (i >= 1)) | (i >= 2)
    _attend(q_ref[...], k_all, v_all, sink_ref, valid, o_ref)


def _attn_prompt(q_rot, k_rot, qkv, sinks, t):
    nc = t // CHUNK
    vcol = OFF_K // LANES
    kv = lambda d: (lambda i: (jnp.maximum(i - d, 0), 0))
    vv = lambda d: (lambda i: (jnp.maximum(i - d, 0), vcol))
    return pl.pallas_call(
        _attn_prompt_body, grid=(nc,),
        in_specs=[pl.BlockSpec(memory_space=pltpu.SMEM),
                  pl.BlockSpec((CHUNK, A_WIDTH), lambda i: (i, 0)),
                  pl.BlockSpec((CHUNK, LANES), kv(2)), pl.BlockSpec((CHUNK, LANES), kv(1)),
                  pl.BlockSpec((CHUNK, LANES), kv(0)),
                  pl.BlockSpec((CHUNK, LANES), vv(2)), pl.BlockSpec((CHUNK, LANES), vv(1)),
                  pl.BlockSpec((CHUNK, LANES), vv(0))],
        out_specs=pl.BlockSpec((CHUNK, A_WIDTH), lambda i: (i, 0)),
        out_shape=jax.ShapeDtypeStruct((t, A_WIDTH), BF16),
        compiler_params=_cp(("parallel",)), name="attn_prompt")(sinks, q_rot, k_rot, k_rot, k_rot, qkv, qkv, qkv)


def _attn_sample_body(sink_ref, q_ref, kn_ref, vn_ref, ck_ref, cv_ref, o_ref, nk_ref, nv_ref):
    s = q_ref.shape[0]
    kn, vn, ck, cv = kn_ref[...], vn_ref[...], ck_ref[0], cv_ref[0]
    k_all = jnp.concatenate([ck, kn], axis=0)
    v_all = jnp.concatenate([cv, vn], axis=0)
    _attend(q_ref[...], k_all.astype(BF16), v_all.astype(BF16), sink_ref, None, o_ref)
    nk_ref[0] = k_all[s:]
    nv_ref[0] = v_all[s:]


def _attn_sample(q_rot, k_rot, qkv, cache_k, cache_v, sinks, row0, nb, s):
    b0 = row0 // s
    vcol = OFF_K // LANES
    return pl.pallas_call(
        _attn_sample_body, grid=(nb,),
        in_specs=[pl.BlockSpec(memory_space=pltpu.SMEM),
                  pl.BlockSpec((s, A_WIDTH), lambda b: (b0 + b, 0)),
                  pl.BlockSpec((s, LANES), lambda b: (b0 + b, 0)),
                  pl.BlockSpec((s, LANES), lambda b: (b0 + b, vcol)),
                  pl.BlockSpec((1, WINDOW, LANES), lambda b: (b, 0, 0)),
                  pl.BlockSpec((1, WINDOW, LANES), lambda b: (b, 0, 0))],
        out_specs=[pl.BlockSpec((s, A_WIDTH), lambda b: (b, 0)),
                   pl.BlockSpec((1, WINDOW, LANES), lambda b: (b, 0, 0)),
                   pl.BlockSpec((1, WINDOW, LANES), lambda b: (b, 0, 0))],
        out_shape=[jax.ShapeDtypeStruct((nb * s, A_WIDTH), BF16),
                   jax.ShapeDtypeStruct((nb, WINDOW, LANES), F32),
                   jax.ShapeDtypeStruct((nb, WINDOW, LANES), F32)],
        compiler_params=_cp(("parallel",)), name="attn_sample")(sinks, q_rot, k_rot, qkv, cache_k, cache_v)


def _rw_prep_body(f_ref, fp_ref, mu_ref, w0_ref, a0_ref, kk_ref, ka_ref, w2_ref, a2_ref, g2_ref,
                  r_o, k_o, v_o, kk_o, a_o, lw_o, g_o, *, rolled):
    f = f_ref[...]
    if rolled:
        first = lax.broadcasted_iota(jnp.int32, f.shape, 0) == 0
        f_prev = jnp.where(first, fp_ref[0], pltpu.roll(f, 1, 0))
    else:
        f_prev = fp_ref[...]
    fs = f + (f_prev - f) * mu_ref[...]
    w = RW_WIDTH
    r, k, v = fs[:, 0:w], fs[:, w:2 * w], fs[:, 2 * w:3 * w]
    x3 = fs[:, 3 * w:]
    lw = jnp.dot(jnp.tanh(x3).astype(BF16), w2_ref[...], preferred_element_type=F32)
    la = jnp.dot(x3.astype(BF16), a2_ref[...], preferred_element_type=F32)
    g = jnp.dot(jax.nn.sigmoid(x3).astype(BF16), g2_ref[...], preferred_element_type=F32)
    z = -(w0_ref[...] + lw)
    w_log = -(jnp.maximum(z, 0.0) + jnp.log1p(jnp.exp(-jnp.abs(z)))) - 0.5
    a = jax.nn.sigmoid(a0_ref[...] + la)
    r_o[...] = r
    v_o[...] = v
    kk_o[...] = k * kk_ref[...]
    k_o[...] = k * (1.0 + (a - 1.0) * ka_ref[...])
    a_o[...] = a
    lw_o[...] = -jnp.exp(w_log)
    g_o[...] = g


def _rw_prep(f, prev, params, row0, m, tm, rolled):
    i0 = row0 // tm
    row = lambda width: pl.BlockSpec((tm, width), lambda i: (i, 0))
    par = lambda shape: pl.BlockSpec(shape, lambda i: (0, 0))
    prev_spec = pl.BlockSpec((1, 1, RW_PAD), lambda i: (i, 0, 0)) if rolled else row(RW_PAD)
    return pl.pallas_call(
        functools.partial(_rw_prep_body, rolled=rolled), grid=(m // tm,),
        in_specs=[pl.BlockSpec((tm, RW_PAD), lambda i: (i0 + i, 0)), prev_spec, par((1, RW_PAD))]
                 + [par((1, RW_WIDTH))] * 4 + [par((RW_LORA_PAD, RW_WIDTH))] * 3,
        out_specs=[row(RW_WIDTH)] * 7,
        out_shape=[jax.ShapeDtypeStruct((m, RW_WIDTH), F32)] * 7,
        compiler_params=_cp(("parallel",)), name="rwkv_prep")(f, prev, *params)


def _split(x):
    hi = x.astype(BF16)
    return hi, (x - hi.astype(F32)).astype(BF16)


_NN = ((1,), (0,))
_NT = ((1,), (1,))
_TN = ((0,), (0,))


def _dot3(a, b, dims):
    (ah, al), (bh, bl) = a, b
    f = lambda x, y: lax.dot_general(x, y, (dims, ((), ())), preferred_element_type=F32)
    return f(ah, bh) + (f(ah, bl) + f(al, bh))


def _rw_scan_body(r_ref, k_ref, v_ref, kk_ref, a_ref, lw_ref, g_ref, rk_ref, lnw_ref, lnb_ref, s0_ref,
                  o_ref, st_ref, s_scr, *, heads):
    c = pl.program_id(2)
    cs = r_ref.shape[0]
    n = RW_HEAD

    @pl.when(c == 0)
    def _():
        s_scr[...] = s0_ref[0]

    row = lax.broadcasted_iota(jnp.int32, (cs, cs), 0)
    col = lax.broadcasted_iota(jnp.int32, (cs, cs), 1)
    incl = row >= col
    strict = row > col
    ones_tril = incl.astype(BF16)
    steps = cs.bit_length() - 1
    hs = range(heads)
    sls = [slice(h * n, (h + 1) * n) for h in hs]
    r, k, v, kk, a, lw = ([ref[:, sl] for sl in sls] for ref in (r_ref, k_ref, v_ref, kk_ref, a_ref, lw_ref))
    s0 = [s_scr[h] for h in hs]
    kkn = [kk[h] / jnp.maximum(jnp.sqrt(jnp.sum(kk[h] * kk[h], axis=-1, keepdims=True)), 1e-12) for h in hs]
    lw_hi = [lw[h].astype(BF16) for h in hs]
    lw_mid = [(lw[h] - lw_hi[h].astype(F32)).astype(BF16) for h in hs]
    lw_lo = [(lw[h] - lw_hi[h].astype(F32) - lw_mid[h].astype(F32)).astype(BF16) for h in hs]
    tri = lambda part: jnp.dot(ones_tril, part, preferred_element_type=F32)
    cum = [tri(lw_lo[h]) + tri(lw_mid[h]) + tri(lw_hi[h]) for h in hs]
    cum_end = [cum[h][cs - 1:cs, :] for h in hs]
    e_neg = [jnp.exp(-cum[h]) for h in hs]
    e_end = [jnp.exp(cum_end[h] - cum[h]) for h in hs]
    a_t = [-kkn[h] * jnp.exp(cum[h] - lw[h]) for h in hs]
    r_t = [r[h] * jnp.exp(cum[h]) for h in hs]
    ar = [_split(jnp.concatenate([a_t[h], r_t[h]], axis=0)) for h in hs]
    b_s = [_split(kkn[h] * a[h] * e_neg[h]) for h in hs]
    k_s = [_split(k[h] * e_neg[h]) for h in hs]
    v_s = [_split(v[h]) for h in hs]
    s0_s = [_split(s0[h]) for h in hs]
    ab_rb = [_dot3(ar[h], b_s[h], _NT) for h in hs]
    ak_rk = [_dot3(ar[h], k_s[h], _NT) for h in hs]
    l_ak = [_split(jnp.where(strict, ak_rk[h][:cs], 0.0)) for h in hs]
    mv = [_dot3(l_ak[h], v_s[h], _NN) for h in hs]
    x = [jnp.concatenate([a_t[h], mv[h]], axis=1) for h in hs]
    lp = [jnp.where(strict, ab_rb[h][:cs], 0.0) for h in hs]
    for i in range(steps):
        lp_s = [_split(lp[h]) for h in hs]
        x = [x[h] + _dot3(lp_s[h], _split(x[h]), _NN) for h in hs]
        if i < steps - 1:
            lp = [_dot3(lp_s[h], lp_s[h], _NN) for h in hs]
    u = [_dot3(_split(x[h][:, :n]), s0_s[h], _NT) + x[h][:, n:] for h in hs]
    l_rb = [_split(jnp.where(incl, ab_rb[h][cs:], 0.0)) for h in hs]
    l_rk = [_split(jnp.where(incl, ak_rk[h][cs:], 0.0)) for h in hs]
    y = [_dot3((ar[h][0][cs:], ar[h][1][cs:]), s0_s[h], _NT) + _dot3(l_rk[h], v_s[h], _NN) for h in hs]
    y = [y[h] + _dot3(l_rb[h], _split(u[h]), _NN) for h in hs]
    uv = [_split(jnp.concatenate([u[h], v[h]], axis=0)) for h in hs]
    bk = [_split(jnp.concatenate([kkn[h] * a[h] * e_end[h], k[h] * e_end[h]], axis=0)) for h in hs]
    for h in hs:
        s_scr[h] = s0[h] * jnp.exp(cum_end[h]) + _dot3(uv[h], bk[h], _TN)
    outs = []
    for h in hs:
        mean = jnp.mean(y[h], axis=-1, keepdims=True)
        var = jnp.mean(jnp.square(y[h] - mean), axis=-1, keepdims=True)
        yn = (y[h] - mean) * lax.rsqrt(var + RW_LN_EPS) * lnw_ref[:, sls[h]] + lnb_ref[:, sls[h]]
        bonus = jnp.sum(r[h] * k[h] * rk_ref[:, sls[h]], axis=-1, keepdims=True) * v[h]
        outs.append((yn + bonus) * g_ref[:, sls[h]])
    o_ref[...] = jnp.concatenate(outs, axis=1).astype(o_ref.dtype)

    @pl.when(c == pl.num_programs(2) - 1)
    def _():
        st_ref[0] = s_scr[...]


def _rw_scan(prep, r_k, ln_w, ln_b, state0, row0, nb, t, cs, heads=8):
    nc = t // cs
    blk0 = row0 // cs
    width = heads * RW_HEAD
    seq = pl.BlockSpec((cs, width), lambda p, b, c: (blk0 + b * nc + c, p))
    par = pl.BlockSpec((1, width), lambda p, b, c: (0, p))
    st = pl.BlockSpec((1, heads, RW_HEAD, RW_HEAD), lambda p, b, c: (b, p, 0, 0))
    return pl.pallas_call(
        functools.partial(_rw_scan_body, heads=heads), grid=(RW_HEADS // heads, nb, nc),
        in_specs=[seq] * 7 + [par] * 3 + [st],
        out_specs=[pl.BlockSpec((cs, width), lambda p, b, c: (b * nc + c, p)), st],
        out_shape=[jax.ShapeDtypeStruct((nb * t, RW_WIDTH), BF16),
                   jax.ShapeDtypeStruct((nb, RW_HEADS, RW_HEAD, RW_HEAD), F32)],
        scratch_shapes=[pltpu.VMEM((heads, RW_HEAD, RW_HEAD), F32)],
        compiler_params=_cp(("parallel", "parallel", "arbitrary")), name=f"rwkv_scan_{cs}")(
            *prep, r_k, ln_w, ln_b, state0)


def _s5_body(u_ref, bm_ref, cm_ref, lp_ref, d_ref, x0_ref, y_ref, xt_ref, x_scr, cin_scr, carry_scr, *, exact):
    i = pl.program_id(2)
    tt = u_ref.shape[0]
    nb = tt // SUBLANES
    w = S5_TILE_ST

    @pl.when(i == 0)
    def _():
        carry_scr[...] = x0_ref[0, 0]

    u = u_ref[...]
    if exact:
        bu = jnp.dot(u, bm_ref[0], precision=HIGHEST, preferred_element_type=F32)
    else:
        bu = jnp.dot(u.astype(BF16), bm_ref[0].astype(BF16), preferred_element_type=F32)
    nct = w // LANES
    for ct in range(2 * nct):
        x_scr[ct] = bu[:, ct * LANES:(ct + 1) * LANES]

    def lam_pow(t, ct):
        return (lp_ref[0, t:t + 1, ct * LANES:(ct + 1) * LANES],
                lp_ref[0, t:t + 1, w + ct * LANES:w + (ct + 1) * LANES])

    def cmul_add(xr, xi, cr, ci, sr, si):
        return xr + cr * sr - ci * si, xi + cr * si + ci * sr

    def rows(ct, t):
        return x_scr.at[ct, pl.ds(t, nb, stride=SUBLANES), :]

    for ct in range(nct):
        lam_r, lam_i = lam_pow(0, ct)
        for t in range(1, SUBLANES):
            nr, ni = cmul_add(rows(ct, t)[...], rows(nct + ct, t)[...], lam_r, lam_i,
                              rows(ct, t - 1)[...], rows(nct + ct, t - 1)[...])
            rows(ct, t)[...] = nr
            rows(nct + ct, t)[...] = ni

    lam8 = jnp.concatenate([lp_ref[0, SUBLANES - 1:SUBLANES, :w], lp_ref[0, SUBLANES - 1:SUBLANES, w:]], axis=0)

    def block_carry(b, carry):
        last = b * SUBLANES + SUBLANES - 1
        for ct in range(2 * nct):
            half, lt = divmod(ct, nct)
            cin_scr[ct, pl.ds(b, 1), :] = carry[half:half + 1, lt * LANES:(lt + 1) * LANES]
        end_r = jnp.concatenate([x_scr[ct, pl.ds(last, 1), :] for ct in range(nct)], axis=1)
        end_i = jnp.concatenate([x_scr[nct + ct, pl.ds(last, 1), :] for ct in range(nct)], axis=1)
        nr, ni = cmul_add(end_r, end_i, lam8[0:1], lam8[1:2], carry[0:1], carry[1:2])
        return jnp.concatenate([nr, ni], axis=0)

    carry0 = jnp.concatenate([carry_scr[:, :w], carry_scr[:, w:]], axis=0)
    carry = lax.fori_loop(0, nb, block_carry, carry0)
    carry_scr[...] = jnp.concatenate([carry[0:1], carry[1:2]], axis=1)
    for ct in range(nct):
        cin_r, cin_i = cin_scr[ct], cin_scr[nct + ct]
        for t in range(SUBLANES):
            pr, pi_ = lam_pow(t, ct)
            nr, ni = cmul_add(rows(ct, t)[...], rows(nct + ct, t)[...], pr, pi_, cin_r, cin_i)
            rows(ct, t)[...] = nr
            rows(nct + ct, t)[...] = ni

    xs = jnp.concatenate([x_scr[ct] for ct in range(2 * nct)], axis=1)
    if exact:
        y = jnp.dot(xs, cm_ref[0], precision=HIGHEST, preferred_element_type=F32)
    else:
        y = jnp.dot(xs.astype(BF16), cm_ref[0].astype(BF16), preferred_element_type=F32)
    y_ref[...] = y + d_ref[...] * u

    @pl.when(i == pl.num_programs(2) - 1)
    def _():
        xt_ref[0, 0] = carry_scr[...]


def _s5_scan(u, bm, cm, lp, d, x0, row0, nb_seq, t, tt, exact):
    nt = t // tt
    i0 = row0 // tt
    w2 = 2 * S5_TILE_ST
    return pl.pallas_call(
        functools.partial(_s5_body, exact=exact), grid=(S5_TILES, nb_seq, nt),
        in_specs=[pl.BlockSpec((tt, S5_TILE_IN), lambda j, b, i: (i0 + b * nt + i, j)),
                  pl.BlockSpec((1, S5_TILE_IN, w2), lambda j, b, i: (j, 0, 0)),
                  pl.BlockSpec((1, w2, S5_TILE_IN), lambda j, b, i: (j, 0, 0)),
                  pl.BlockSpec((1, SUBLANES, w2), lambda j, b, i: (j, 0, 0)),
                  pl.BlockSpec((1, S5_TILE_IN), lambda j, b, i: (0, j)),
                  pl.BlockSpec((1, 1, 1, w2), lambda j, b, i: (b, j, 0, 0))],
        out_specs=[pl.BlockSpec((tt, S5_TILE_IN), lambda j, b, i: (b * nt + i, j)),
                   pl.BlockSpec((1, 1, 1, w2), lambda j, b, i: (b, j, 0, 0))],
        out_shape=[jax.ShapeDtypeStruct((nb_seq * t, S5_WIDTH), F32),
                   jax.ShapeDtypeStruct((nb_seq, S5_TILES, 1, w2), F32)],
        scratch_shapes=[pltpu.VMEM((w2 // LANES, tt, LANES), F32), pltpu.VMEM((w2 // LANES, tt // SUBLANES, LANES), F32),
                        pltpu.VMEM((1, w2), F32)],
        compiler_params=_cp(("parallel", "parallel", "arbitrary")), name=f"s5_scan_{tt}")(u, bm, cm, lp, d, x0)


def _gelu_tanh(y):
    return 0.5 * y * (1.0 + jnp.tanh(math.sqrt(2.0 / math.pi) * (y + 0.044715 * (y * y * y))))


def _s5_glu_body(y_ref, w_ref, b_ref, o_ref, *, tn):
    j = pl.program_id(1)
    z = _gelu_tanh(y_ref[...])
    gate = jax.nn.sigmoid(jnp.dot(z.astype(BF16), w_ref[...], preferred_element_type=F32) + b_ref[...])
    zj = _gelu_tanh(y_ref[:, pl.ds(pl.multiple_of(j * tn, tn), tn)])
    o_ref[...] = (zj * gate).astype(o_ref.dtype)


def _s5_glu(y, w, b, tn=512):
    m, d = y.shape
    tm = _pick(m, (256, 128, 16, 8))
    return pl.pallas_call(
        functools.partial(_s5_glu_body, tn=tn), grid=(m // tm, d // tn),
        in_specs=[pl.BlockSpec((tm, d), lambda i, j: (i, 0)), pl.BlockSpec((d, tn), lambda i, j: (0, j)),
                  pl.BlockSpec((1, tn), lambda i, j: (0, j))],
        out_specs=pl.BlockSpec((tm, tn), lambda i, j: (i, j)),
        out_shape=jax.ShapeDtypeStruct((m, d), BF16),
        compiler_params=_cp(("parallel", "parallel")), name="s5_glu")(y, w, b)


def _merge_body(oa_ref, ob_ref, oc_ref, ga_ref, gb_ref, gc_ref, wa_ref, wb_ref, wc_ref, o_ref):
    acc = ga_ref[...] * jnp.dot(oa_ref[...], wa_ref[...], preferred_element_type=F32)
    acc = acc + gb_ref[...] * jnp.dot(ob_ref[...], wb_ref[...], preferred_element_type=F32)
    acc = acc + gc_ref[...] * jnp.dot(oc_ref[...], wc_ref[...], preferred_element_type=F32)
    o_ref[...] = acc.astype(o_ref.dtype)


def _merge(o_a, o_b, o_c, gates, w_a, w_b, w_c, tn=512):
    m = o_a.shape[0]
    tm = _row_tile(m)
    nj = D_MODEL // tn
    full = lambda width: pl.BlockSpec((tm, width), lambda i, j: (i, 0))
    gate = lambda which: pl.BlockSpec((tm, tn), lambda i, j: (i, which * nj + j))
    wcol = lambda rows: pl.BlockSpec((rows, tn), lambda i, j: (0, j))
    return pl.pallas_call(
        _merge_body, grid=(m // tm, nj),
        in_specs=[full(A_WIDTH), full(RW_WIDTH), full(S5_WIDTH), gate(0), gate(1), gate(2),
                  wcol(A_WIDTH), wcol(RW_WIDTH), wcol(S5_WIDTH)],
        out_specs=pl.BlockSpec((tm, tn), lambda i, j: (i, j)),
        out_shape=jax.ShapeDtypeStruct((m, D_MODEL), BF16),
        compiler_params=_cp(("parallel", "parallel")), name="merge")(o_a, o_b, o_c, gates, gates, gates, w_a, w_b, w_c)


def _expert_body(te_ref, nu_ref, x_ref, wg_ref, wu_ref, wd_ref, o_ref):
    i, j = pl.program_id(0), pl.program_id(1)
    used = i < nu_ref[0]

    @pl.when(jnp.logical_and(j == 0, jnp.logical_not(used)))
    def _():
        o_ref[...] = jnp.zeros_like(o_ref)

    @pl.when(used)
    def _():
        x = x_ref[...]
        hg = jnp.dot(x, wg_ref[...].astype(BF16), preferred_element_type=F32)
        hu = jnp.dot(x, wu_ref[...].astype(BF16), preferred_element_type=F32)
        h = (hg * jax.nn.sigmoid(hg) * hu).astype(BF16)
        part = jnp.dot(h, wd_ref[...].astype(BF16), preferred_element_type=F32)

        @pl.when(j == 0)
        def _():
            o_ref[...] = part

        @pl.when(j > 0)
        def _():
            o_ref[...] += part


def _experts(xg, tile_expert, n_used, layer, w_gate, w_up, w_down):
    rows, d = xg.shape
    n_tiles = rows // MOE_TM
    nf = D_EXPERT // MOE_TF

    def wmap(i, j, te, nu):
        live = i < nu[0]
        return layer, te[jnp.minimum(i, nu[0] - 1)], 0, jnp.where(live, j, nf - 1)

    def wdmap(i, j, te, nu):
        live = i < nu[0]
        return layer, te[jnp.minimum(i, nu[0] - 1)], jnp.where(live, j, nf - 1), 0

    grid_spec = pltpu.PrefetchScalarGridSpec(
        num_scalar_prefetch=2, grid=(n_tiles, nf),
        in_specs=[pl.BlockSpec((MOE_TM, d), lambda i, j, te, nu: (i, 0)),
                  pl.BlockSpec((None, None, d, MOE_TF), wmap), pl.BlockSpec((None, None, d, MOE_TF), wmap),
                  pl.BlockSpec((None, None, MOE_TF, d), wdmap)],
        out_specs=pl.BlockSpec((MOE_TM, d), lambda i, j, te, nu: (i, 0)))
    return pl.pallas_call(
        _expert_body, grid_spec=grid_spec, out_shape=jax.ShapeDtypeStruct((rows, d), F32),
        compiler_params=_cp(("parallel", "arbitrary"), 56), name="experts")(
            tile_expert, n_used, xg, w_gate, w_up, w_down)


def _moe(xn, rw, rg_b, re_b, layer, w_gate, w_up, w_down):
    n, d = xn.shape
    logits = _matmul(xn, rw, tn=LANES, name="router")
    g_logits = logits[:, :N_GROUPS] + rg_b
    grp = jnp.argmax(g_logits, axis=-1).astype(jnp.int32)
    p_grp = jnp.take_along_axis(jax.nn.softmax(g_logits, axis=-1), grp[:, None], axis=-1)
    e_logits = (logits[:, N_GROUPS:N_GROUPS + N_EXPERTS] + re_b).reshape(n, N_GROUPS, EXPERTS_PER_GROUP)
    e_logits = jnp.take_along_axis(e_logits, grp[:, None, None], axis=1)[:, 0]
    top_v, top_i = lax.top_k(e_logits, TOP_K)
    gate = (jax.nn.softmax(top_v, axis=-1) * p_grp).reshape(-1)
    expert_id = (grp[:, None] * EXPERTS_PER_GROUP + top_i).astype(jnp.int32).reshape(-1)

    a = n * TOP_K
    n_tiles = -(-(a + N_EXPERTS * (MOE_TM - 1)) // MOE_TM)
    idx = jnp.arange(a, dtype=jnp.int32)
    order = jnp.argsort(expert_id * a + idx)
    e_sorted = expert_id[order]
    counts = jnp.bincount(expert_id, length=N_EXPERTS).astype(jnp.int32)
    starts = jnp.cumsum(counts) - counts
    padded = (counts + MOE_TM - 1) // MOE_TM * MOE_TM
    pad_end = jnp.cumsum(padded)
    pad_start = pad_end - padded
    slot_sorted = (pad_start[e_sorted] + idx - starts[e_sorted]).astype(jnp.int32)
    slot = jnp.zeros((a,), jnp.int32).at[order].set(slot_sorted)
    slot_tok = jnp.full((n_tiles * MOE_TM,), n, jnp.int32).at[slot].set(idx // TOP_K)
    tile_expert = jnp.minimum(
        jnp.searchsorted(pad_end, jnp.arange(n_tiles, dtype=jnp.int32) * MOE_TM, side='right'),
        N_EXPERTS - 1).astype(jnp.int32)
    n_used = (pad_end[-1:] // MOE_TM).astype(jnp.int32)
    x_pad = jnp.concatenate([xn, jnp.zeros((1, d), xn.dtype)], axis=0)
    out = _experts(x_pad[slot_tok], tile_expert, n_used, layer, w_gate, w_up, w_down)
    contrib = out[slot] * gate[:, None]
    return jnp.sum(contrib.reshape(n, TOP_K, d), axis=1)


def _s5_params(lam_re, lam_im, b_re, b_im, c_re, c_im, log_step):
    dt = jnp.exp(log_step)[:, None]
    mag = jnp.exp(lam_re * dt)
    lbr, lbi = mag * jnp.cos(lam_im * dt), mag * jnp.sin(lam_im * dt)
    den = lam_re * lam_re + lam_im * lam_im
    qr = ((lbr - 1.0) * lam_re + lbi * lam_im) / den
    qi = (lbi * lam_re - (lbr - 1.0) * lam_im) / den
    bbr = qr[:, :, None] * b_re - qi[:, :, None] * b_im
    bbi = qr[:, :, None] * b_im + qi[:, :, None] * b_re
    eye = jnp.eye(S5_TILE_GROUPS, dtype=F32)

    def blockdiag_in(m):
        m = m.reshape(S5_TILES, S5_TILE_GROUPS, S5_STATE, S5_GROUP)
        return jnp.einsum('jgph,ge->jghep', m, eye).reshape(S5_TILES, S5_TILE_IN, S5_TILE_ST)

    def blockdiag_out(m):
        m = m.reshape(S5_TILES, S5_TILE_GROUPS, S5_GROUP, S5_STATE)
        return jnp.einsum('jghp,ge->jgpeh', m, eye).reshape(S5_TILES, S5_TILE_ST, S5_TILE_IN)

    bm = jnp.concatenate([blockdiag_in(bbr), blockdiag_in(bbi)], axis=2)
    cm = jnp.concatenate([blockdiag_out(c_re), blockdiag_out(-c_im)], axis=1)
    pr, pi_ = [lbr], [lbi]
    for _ in range(SUBLANES - 1):
        pr, pi_ = pr + [pr[-1] * lbr - pi_[-1] * lbi], pi_ + [pr[-1] * lbi + pi_[-1] * lbr]
    tile = lambda p: jnp.stack(p, 0).reshape(SUBLANES, S5_TILES, S5_TILE_ST).transpose(1, 0, 2)
    lp = jnp.concatenate([tile(pr), tile(pi_)], axis=2)
    return bm, cm, lp


def _s5_state_in(re, im):
    b = re.shape[0]
    f = lambda x: x.reshape(b, S5_TILES, 1, S5_TILE_ST)
    return jnp.concatenate([f(re), f(im)], axis=3)


def _s5_state_out(x):
    b = x.shape[0]
    return (x[..., 0, :S5_TILE_ST].reshape(b, S5_GROUPS, S5_STATE), x[..., 0, S5_TILE_ST:].reshape(b, S5_GROUPS, S5_STATE))


def _pad_rows(w, rows, at):
    return jnp.zeros((rows, w.shape[1]), w.dtype).at[at:at + w.shape[0]].set(w)


def _layer(x, tp, nbs, ts, rope_cos, rope_sin, cache_k, cache_v, shift0, rw0, s5re0, s5im0, lw, experts):
    n = x.shape[0]
    bf = lambda w: w.astype(BF16)
    w_in = lw['w_in']
    w_pk = jnp.concatenate([bf(w_in[:, :OFF_RW]), jnp.zeros((D_MODEL, RW_PAD - SHIFT_WIDTH), BF16),
                            bf(w_in[:, OFF_RW:])], axis=1)
    pk_s5 = OFF_V + RW_PAD
    xn = _rmsnorm(x, lw['norm_mix_g'], BF16)
    qkv = _matmul(xn, w_pk, col0=0, n=OFF_V, tn=256, name="proj_qkv")
    f = _matmul(xn, w_pk, col0=OFF_V, n=RW_PAD, tn=256, name="proj_rwkv")
    u = _matmul(xn, w_pk, col0=pk_s5, n=S5_WIDTH, name="proj_s5")
    gates = _matmul(xn, w_pk, col0=pk_s5 + S5_WIDTH, n=3 * D_MODEL, act="sigmoid", name="proj_gates")

    q_rot, k_rot = _rope(qkv, rope_cos, rope_sin)
    o_ap = _attn_prompt(q_rot, k_rot, qkv, lw['attn_sinks'], tp)
    ck = cache_k.reshape(nbs, WINDOW, A_KV_WIDTH)
    cv = cache_v.reshape(nbs, WINDOW, A_KV_WIDTH)
    o_as, nk_s, nv_s = _attn_sample(q_rot, k_rot, qkv, ck, cv, lw['attn_sinks'], tp, nbs, ts)
    o_a = jnp.concatenate([o_ap, o_as], axis=0)
    kv_shape = (-1, WINDOW, A_KV_HEADS, HEAD_DIM)
    nk_p = k_rot[tp - WINDOW:tp].reshape(kv_shape)
    nv_p = qkv[tp - WINDOW:tp, OFF_K:OFF_V].reshape(kv_shape)

    ms = nbs * ts
    tm_p = _pick(tp, (256, 128, 64))
    assert tp % ms == 0, "sample rows must start on a tile boundary"
    above = jnp.concatenate([jnp.zeros((1, RW_PAD), F32), f[tm_p - 1:tp - 1:tm_p]], axis=0)[:, None, :]
    f_s = f[tp:].reshape(nbs, ts, RW_PAD)
    sh = jnp.pad(shift0, ((0, 0), (0, RW_PAD - SHIFT_WIDTH)))
    fp_s = jnp.concatenate([sh[:, None, :], f_s[:, :-1]], axis=1).reshape(ms, RW_PAD)
    row = lambda v: v.reshape(1, -1)
    prep_par = (jnp.pad(row(lw['rwkv_mu']), ((0, 0), (0, RW_PAD - SHIFT_WIDTH))), row(lw['rwkv_w0']),
                row(lw['rwkv_a0']), row(lw['rwkv_k_k']), row(lw['rwkv_k_a']),
                bf(_pad_rows(lw['rwkv_w2'], RW_LORA_PAD, 0)), bf(_pad_rows(lw['rwkv_a2'], RW_LORA_PAD, RW_LORA)),
                bf(_pad_rows(lw['rwkv_g2'], RW_LORA_PAD, 2 * RW_LORA)))
    prep_p = _rw_prep(f, above, prep_par, 0, tp, tm_p, True)
    prep_s = _rw_prep(f, fp_s, prep_par, tp, ms, ms, False)
    rw_par = (row(lw['rwkv_r_k']), row(lw['rwkv_ln_w']), row(lw['rwkv_ln_b']))
    zero_rw = jnp.zeros((1, RW_HEADS, RW_HEAD, RW_HEAD), F32)
    o_bp, rw_p = _rw_scan(prep_p, *rw_par, zero_rw, 0, 1, tp, CHUNK)
    o_bs, rw_s = _rw_scan(prep_s, *rw_par, rw0, 0, nbs, ts, ts)
    o_b = jnp.concatenate([o_bp, o_bs], axis=0)
    shift_p = f[tp - 1:tp, :SHIFT_WIDTH]
    shift_s = f_s[:, -1, :SHIFT_WIDTH]

    bm, cm, lp = _s5_params(lw['s5_lambda_re'], lw['s5_lambda_im'], lw['s5_b_re'], lw['s5_b_im'],
                            lw['s5_c_re'], lw['s5_c_im'], lw['s5_log_step'])
    d_row = row(lw['s5_d'])
    zero_s5 = jnp.zeros((1, S5_TILES, 1, 2 * S5_TILE_ST), F32)
    y_p, xs_p = _s5_scan(u, bm, cm, lp, d_row, zero_s5, 0, 1, tp, tm_p, False)
    y_s, xs_s = _s5_scan(u, bm, cm, lp, d_row, _s5_state_in(s5re0, s5im0), tp, nbs, ts, ts, True)
    o_c = _s5_glu(jnp.concatenate([y_p, y_s], axis=0), bf(lw['s5_glu_w']), row(lw['s5_glu_b']))
    s5re_p, s5im_p = _s5_state_out(xs_p)
    s5re_s, s5im_s = _s5_state_out(xs_s)

    merged = _merge(o_a, o_b, o_c, gates, bf(lw['w_branch_attn']), bf(lw['w_branch_rwkv']), bf(lw['w_branch_s5']))
    x = _matmul_residual(merged, bf(lw['w_out']), x)

    xn2 = _rmsnorm(x, lw['norm_ffn_g'], BF16)
    rw = jnp.concatenate([lw['router_group_w'], lw['router_expert_w'],
                          jnp.zeros((D_MODEL, LANES - N_GROUPS - N_EXPERTS), F32)], axis=1)
    x = x + _moe(xn2, bf(rw), lw['router_group_b'], lw['router_expert_b'], *experts)
    st_p = (nk_p, nv_p, shift_p, rw_p, s5re_p, s5im_p)
    st_s = (nk_s.reshape(kv_shape), nv_s.reshape(kv_shape), shift_s, rw_s, s5re_s, s5im_s)
    return x, st_p, st_s


def _rope_tables(pos):
    half = HEAD_DIM // 2
    inv_freq = ROPE_THETA ** (-jnp.arange(half, dtype=F32) / half)
    ang = pos.astype(F32)[:, None] * inv_freq[None, :]
    cos, sin = jnp.cos(ang), jnp.sin(ang)
    reps = LANES // HEAD_DIM
    return jnp.tile(jnp.concatenate([cos, cos], axis=1), (1, reps)), jnp.tile(jnp.concatenate([-sin, sin], axis=1), (1, reps))


_LAYER_KEYS = ('norm_mix_g', 'w_in', 'attn_sinks', 'rwkv_mu', 'rwkv_w0', 'rwkv_w2', 'rwkv_a0', 'rwkv_a2', 'rwkv_g2',
               'rwkv_k_k', 'rwkv_k_a', 'rwkv_r_k', 'rwkv_ln_w', 'rwkv_ln_b', 's5_lambda_re', 's5_lambda_im',
               's5_b_re', 's5_b_im', 's5_c_re', 's5_c_im', 's5_d', 's5_log_step', 's5_glu_w', 's5_glu_b',
               'w_branch_attn', 'w_branch_rwkv', 'w_branch_s5', 'w_out', 'norm_ffn_g', 'router_group_w',
               'router_group_b', 'router_expert_w', 'router_expert_b', 'expert_w_gate', 'expert_w_up', 'expert_w_down')


def _forward(x_prompt, x_sample, cache_k, cache_v, state_shift, state_rwkv, state_s5_re, state_s5_im,
             layer_weights, norm_final_g):
    bp, tp, d = x_prompt.shape
    assert bp == 1, "the prompt batch is one new stream"
    nbs, ts, _ = x_sample.shape
    depth = cache_k.shape[0]
    x = jnp.concatenate([x_prompt.reshape(tp, d), x_sample.reshape(nbs * ts, d)], axis=0)
    pos = jnp.concatenate([jnp.arange(tp, dtype=jnp.int32),
                           jnp.tile(PAST_LEN + jnp.arange(ts, dtype=jnp.int32), nbs)])
    rope_cos, rope_sin = _rope_tables(pos)
    new_p = [[] for _ in range(6)]
    new_s = [[] for _ in range(6)]
    for l in range(depth):
        lw = {k: v[l] for k, v in zip(_LAYER_KEYS, layer_weights) if not k.startswith('expert_w_')}
        x, st_p, st_s = _layer(x, tp, nbs, ts, rope_cos, rope_sin, cache_k[l], cache_v[l], state_shift[l],
                               state_rwkv[l], state_s5_re[l], state_s5_im[l], lw, (l,) + tuple(layer_weights[-3:]))
        for i in range(6):
            new_p[i].append(st_p[i])
            new_s[i].append(st_s[i])
    y = _rmsnorm(x, norm_final_g, F32)
    outs_p = tuple(jnp.stack(t, axis=0) for t in new_p)
    outs_s = tuple(jnp.stack(t, axis=0) for t in new_s)
    return (y[:tp].reshape(1, tp, d), y[tp:].reshape(nbs, ts, d)) + outs_p + outs_s


def kernel(x_prompt, x_sample, cache_k, cache_v, state_shift, state_rwkv, state_s5_re, state_s5_im,
           norm_mix_g, w_in, attn_sinks, rwkv_mu, rwkv_w0, rwkv_w2, rwkv_a0, rwkv_a2, rwkv_g2,
           rwkv_k_k, rwkv_k_a, rwkv_r_k, rwkv_ln_w, rwkv_ln_b, s5_lambda_re, s5_lambda_im,
           s5_b_re, s5_b_im, s5_c_re, s5_c_im, s5_d, s5_log_step, s5_glu_w, s5_glu_b,
           w_branch_attn, w_branch_rwkv, w_branch_s5, w_out, norm_ffn_g, router_group_w,
           router_group_b, router_expert_w, router_expert_b, expert_w_gate, expert_w_up,
           expert_w_down, norm_final_g):
    layer_weights = (norm_mix_g, w_in, attn_sinks, rwkv_mu, rwkv_w0, rwkv_w2, rwkv_a0, rwkv_a2, rwkv_g2,
                     rwkv_k_k, rwkv_k_a, rwkv_r_k, rwkv_ln_w, rwkv_ln_b, s5_lambda_re, s5_lambda_im,
                     s5_b_re, s5_b_im, s5_c_re, s5_c_im, s5_d, s5_log_step, s5_glu_w, s5_glu_b,
                     w_branch_attn, w_branch_rwkv, w_branch_s5, w_out, norm_ffn_g, router_group_w,
                     router_group_b, router_expert_w, router_expert_b, expert_w_gate, expert_w_up, expert_w_down)
    return _forward(x_prompt, x_sample, cache_k, cache_v, state_shift, state_rwkv, state_s5_re, state_s5_im,
                    layer_weights, norm_final_g)
```

```python
import functools
import math

import jax
import jax.numpy as jnp
from jax import lax
from jax.experimental import pallas as pl
from jax.experimental.pallas import tpu as pltpu

F32 = jnp.float32
BF16 = jnp.bfloat16
HIGHEST = lax.Precision.HIGHEST

D_MODEL = 4096
CHUNK = 64
NORM_EPS = 1e-5
A_HEADS = 16
A_KV_HEADS = 2
HEAD_DIM = 64
A_WIDTH = A_HEADS * HEAD_DIM
A_KV_WIDTH = A_KV_HEADS * HEAD_DIM
WINDOW = 128
ROPE_THETA = 10000.0
ATTN_SCALE = HEAD_DIM ** -0.5
PAST_LEN = 4096
RW_HEAD = 64
RW_HEADS = 24
RW_WIDTH = RW_HEADS * RW_HEAD
RW_LORA = 64
RW_LN_EPS = 64e-5
SHIFT_WIDTH = 3 * RW_WIDTH + 3 * RW_LORA
S5_GROUPS = 96
S5_GROUP = 16
S5_WIDTH = S5_GROUPS * S5_GROUP
S5_STATE = 64
OFF_Q = A_WIDTH
OFF_K = OFF_Q + A_KV_WIDTH
OFF_V = OFF_K + A_KV_WIDTH
OFF_RW = OFF_V + SHIFT_WIDTH
OFF_S5 = OFF_RW + S5_WIDTH
OFF_GA = OFF_S5 + D_MODEL
OFF_GB = OFF_GA + D_MODEL
IN_COLS = OFF_GB + D_MODEL
N_GROUPS = 4
EXPERTS_PER_GROUP = 8
N_EXPERTS = N_GROUPS * EXPERTS_PER_GROUP
TOP_K = 2
D_EXPERT = 1024

LANES = 128
SUBLANES = 8
RW_PAD = 4864
RW_LORA_PAD = RW_PAD - 3 * RW_WIDTH
S5_TILE_GROUPS = 8
S5_TILE_IN = S5_TILE_GROUPS * S5_GROUP
S5_TILE_ST = S5_TILE_GROUPS * S5_STATE
S5_TILES = S5_GROUPS // S5_TILE_GROUPS
MOE_TM = 640
MOE_TK = 512
MOE_KSTEPS = D_MODEL // MOE_TK
MOE_TN = 512


def _cp(sem, vmem_mb=48):
    return pltpu.CompilerParams(dimension_semantics=sem, vmem_limit_bytes=vmem_mb * 2 ** 20)


def _pick(n, prefs):
    for p in prefs:
        if n % p == 0:
            return p
    raise ValueError(f"no tile for {n} in {prefs}")


def _rmsnorm_body(x_ref, g_ref, o_ref):
    x = x_ref[...]
    inv = lax.rsqrt(jnp.mean(x * x, axis=-1, keepdims=True) + NORM_EPS)
    o_ref[...] = (x * inv * g_ref[...]).astype(o_ref.dtype)


def _rmsnorm(x, g, out_dtype):
    m, d = x.shape
    tm = _pick(m, (256, 128, 16, 8))
    return pl.pallas_call(
        _rmsnorm_body, grid=(m // tm,),
        in_specs=[pl.BlockSpec((tm, d), lambda i: (i, 0)), pl.BlockSpec((1, d), lambda i: (0, 0))],
        out_specs=pl.BlockSpec((tm, d), lambda i: (i, 0)),
        out_shape=jax.ShapeDtypeStruct((m, d), out_dtype),
        compiler_params=_cp(("parallel",)), name="rmsnorm")(x, g.reshape(1, d))


def _mm_body(a_ref, b_ref, o_ref, *, act):
    acc = jnp.dot(a_ref[...].astype(BF16), b_ref[...], preferred_element_type=F32)
    if act == "sigmoid":
        acc = jax.nn.sigmoid(acc)
    o_ref[...] = acc.astype(o_ref.dtype)


def _mm_res_body(a_ref, b_ref, r_ref, o_ref):
    o_ref[...] = r_ref[...] + jnp.dot(a_ref[...], b_ref[...], preferred_element_type=F32)


def _row_tile(m):
    return _pick(m, (768, 512, 256, 128, 64, 16, 8))


def _matmul(a, b, *, col0=0, n=None, act=None, out_dtype=F32, tn=512, name="matmul"):
    m, k = a.shape
    n = b.shape[1] if n is None else n
    tm = _row_tile(m)
    tn = _pick(math.gcd(n, col0) if col0 else n, (tn, 256, 128))
    j0 = col0 // tn
    return pl.pallas_call(
        functools.partial(_mm_body, act=act), grid=(m // tm, n // tn),
        in_specs=[pl.BlockSpec((tm, k), lambda i, j: (i, 0)), pl.BlockSpec((k, tn), lambda i, j: (0, j0 + j))],
        out_specs=pl.BlockSpec((tm, tn), lambda i, j: (i, j)),
        out_shape=jax.ShapeDtypeStruct((m, n), out_dtype),
        compiler_params=_cp(("parallel", "parallel")), name=name)(a, b)


def _matmul_residual(a, b, res, *, tn=512):
    m, k = a.shape
    n = b.shape[1]
    tm = _row_tile(m)
    return pl.pallas_call(
        _mm_res_body, grid=(m // tm, n // tn),
        in_specs=[pl.BlockSpec((tm, k), lambda i, j: (i, 0)), pl.BlockSpec((k, tn), lambda i, j: (0, j)),
                  pl.BlockSpec((tm, tn), lambda i, j: (i, j))],
        out_specs=pl.BlockSpec((tm, tn), lambda i, j: (i, j)),
        out_shape=jax.ShapeDtypeStruct((m, n), F32),
        compiler_params=_cp(("parallel", "parallel")), name="matmul_residual")(a, b, res)


def _repack_body(w_ref, o_ref):
    w = w_ref[...]
    pad = jnp.zeros((w.shape[0], RW_PAD - SHIFT_WIDTH), F32)
    o_ref[...] = jnp.concatenate([w[:, :OFF_RW], pad, w[:, OFF_RW:]], axis=1).astype(o_ref.dtype)


def _repack_w_in(w_in_all, layer):
    _, k, n = w_in_all.shape
    tr = 64
    n_out = n + RW_PAD - SHIFT_WIDTH
    return pl.pallas_call(
        _repack_body, grid=(k // tr,),
        in_specs=[pl.BlockSpec((None, tr, n), lambda i: (layer, i, 0))],
        out_specs=pl.BlockSpec((tr, n_out), lambda i: (i, 0)),
        out_shape=jax.ShapeDtypeStruct((k, n_out), BF16),
        compiler_params=_cp(("parallel",)), name="repack_w_in")(w_in_all)


def _rope_body(x_ref, cos_ref, sin_ref, q_ref, k_ref):
    cos = cos_ref[...]
    sin = sin_ref[...]
    lane = lax.broadcasted_iota(jnp.int32, cos.shape, 1)
    first = (lane % HEAD_DIM) < (HEAD_DIM // 2)
    for j in range((A_WIDTH + A_KV_WIDTH) // LANES):
        x = x_ref[:, j * LANES:(j + 1) * LANES]
        partner = jnp.where(first, pltpu.roll(x, LANES - HEAD_DIM // 2, 1), pltpu.roll(x, HEAD_DIM // 2, 1))
        y = x * cos + partner * sin
        if j < A_WIDTH // LANES:
            q_ref[:, j * LANES:(j + 1) * LANES] = y
        else:
            k_ref[...] = y


def _rope(qkv, cos, sin):
    m, w = qkv.shape
    tm = _pick(m, (256, 128, 16, 8))
    return pl.pallas_call(
        _rope_body, grid=(m // tm,),
        in_specs=[pl.BlockSpec((tm, w), lambda i: (i, 0)), pl.BlockSpec((tm, LANES), lambda i: (i, 0)),
                  pl.BlockSpec((tm, LANES), lambda i: (i, 0))],
        out_specs=[pl.BlockSpec((tm, A_WIDTH), lambda i: (i, 0)), pl.BlockSpec((tm, LANES), lambda i: (i, 0))],
        out_shape=[jax.ShapeDtypeStruct((m, A_WIDTH), F32), jax.ShapeDtypeStruct((m, LANES), F32)],
        compiler_params=_cp(("parallel",)), name="rope")(qkv, cos, sin)


def _attend(q, k_all, v_all, sink_ref, valid, o_ref):
    rows = q.shape[0]
    grp = A_HEADS // A_KV_HEADS
    outs = []
    for h in range(A_KV_HEADS):
        kh = k_all[:, h * HEAD_DIM:(h + 1) * HEAD_DIM]
        vh = v_all[:, h * HEAD_DIM:(h + 1) * HEAD_DIM]
        qs = jnp.concatenate(
            [q[:, (h * grp + g) * HEAD_DIM:(h * grp + g + 1) * HEAD_DIM] for g in range(grp)], axis=0).astype(BF16)
        s = lax.dot_general(qs, kh, (((1,), (1,)), ((), ())), preferred_element_type=F32) * ATTN_SCALE
        if valid is not None:
            s = jnp.where(valid, s, -jnp.inf)
        sink = jnp.concatenate([jnp.full((rows, 1), sink_ref[h * grp + g], F32) for g in range(grp)], axis=0)
        m = jnp.maximum(jnp.max(s, axis=-1, keepdims=True), sink)
        p = jnp.exp(s - m)
        den = jnp.sum(p, axis=-1, keepdims=True) + jnp.exp(sink - m)
        o = jnp.dot((p / den).astype(BF16), vh, preferred_element_type=F32)
        outs += [o[g * rows:(g + 1) * rows] for g in range(grp)]
    o_ref[...] = jnp.concatenate(outs, axis=1).astype(o_ref.dtype)


def _attn_prompt_body(sink_ref, q_ref, k0_ref, k1_ref, k2_ref, v0_ref, v1_ref, v2_ref, o_ref):
    i = pl.program_id(0)
    k_all = jnp.concatenate([k0_ref[...], k1_ref[...], k2_ref[...]], axis=0).astype(BF16)
    v_all = jnp.concatenate([v0_ref[...], v1_ref[...], v2_ref[...]], axis=0).astype(BF16)
    col = lax.broadcasted_iota(jnp.int32, (1, 3 * CHUNK), 1)
    valid = (col >= 2 * CHUNK) | ((col >= CHUNK) & (i >= 1)) | (i >= 2)
    _attend(q_ref[...], k_all, v_all, sink_ref, valid, o_ref)


def _attn_prompt(q_rot, k_rot, qkv, sinks, t):
    nc = t // CHUNK
    vcol = OFF_K // LANES
    kv = lambda d: (lambda i: (jnp.maximum(i - d, 0), 0))
    vv = lambda d: (lambda i: (jnp.maximum(i - d, 0), vcol))
    return pl.pallas_call(
        _attn_prompt_body, grid=(nc,),
        in_specs=[pl.BlockSpec(memory_space=pltpu.SMEM),
                  pl.BlockSpec((CHUNK, A_WIDTH), lambda i: (i, 0)),
                  pl.BlockSpec((CHUNK, LANES), kv(2)), pl.BlockSpec((CHUNK, LANES), kv(1)),
                  pl.BlockSpec((CHUNK, LANES), kv(0)),
                  pl.BlockSpec((CHUNK, LANES), vv(2)), pl.BlockSpec((CHUNK, LANES), vv(1)),
                  pl.BlockSpec((CHUNK, LANES), vv(0))],
        out_specs=pl.BlockSpec((CHUNK, A_WIDTH), lambda i: (i, 0)),
        out_shape=jax.ShapeDtypeStruct((t, A_WIDTH), BF16),
        compiler_params=_cp(("parallel",)), name="attn_prompt")(sinks, q_rot, k_rot, k_rot, k_rot, qkv, qkv, qkv)


def _attn_sample_body(sink_ref, q_ref, kn_ref, vn_ref, ck_ref, cv_ref, o_ref, nk_ref, nv_ref):
    s = q_ref.shape[0]
    kn, vn, ck, cv = kn_ref[...], vn_ref[...], ck_ref[0], cv_ref[0]
    k_all = jnp.concatenate([ck, kn], axis=0)
    v_all = jnp.concatenate([cv, vn], axis=0)
    _attend(q_ref[...], k_all.astype(BF16), v_all.astype(BF16), sink_ref, None, o_ref)
    nk_ref[0] = k_all[s:]
    nv_ref[0] = v_all[s:]


def _attn_sample(q_rot, k_rot, qkv, cache_k, cache_v, sinks, row0, nb, s):
    b0 = row0 // s
    vcol = OFF_K // LANES
    return pl.pallas_call(
        _attn_sample_body, grid=(nb,),
        in_specs=[pl.BlockSpec(memory_space=pltpu.SMEM),
                  pl.BlockSpec((s, A_WIDTH), lambda b: (b0 + b, 0)),
                  pl.BlockSpec((s, LANES), lambda b: (b0 + b, 0)),
                  pl.BlockSpec((s, LANES), lambda b: (b0 + b, vcol)),
                  pl.BlockSpec((1, WINDOW, LANES), lambda b: (b, 0, 0)),
                  pl.BlockSpec((1, WINDOW, LANES), lambda b: (b, 0, 0))],
        out_specs=[pl.BlockSpec((s, A_WIDTH), lambda b: (b, 0)),
                   pl.BlockSpec((1, WINDOW, LANES), lambda b: (b, 0, 0)),
                   pl.BlockSpec((1, WINDOW, LANES), lambda b: (b, 0, 0))],
        out_shape=[jax.ShapeDtypeStruct((nb * s, A_WIDTH), BF16),
                   jax.ShapeDtypeStruct((nb, WINDOW, LANES), F32),
                   jax.ShapeDtypeStruct((nb, WINDOW, LANES), F32)],
        compiler_params=_cp(("parallel",)), name="attn_sample")(sinks, q_rot, k_rot, qkv, cache_k, cache_v)


def _rw_prep_body(f_ref, fp_ref, mu_ref, w0_ref, a0_ref, kk_ref, ka_ref, w2_ref, a2_ref, g2_ref,
                  r_o, k_o, v_o, kk_o, a_o, lw_o, g_o, *, rolled):
    f = f_ref[...]
    if rolled:
        first = lax.broadcasted_iota(jnp.int32, f.shape, 0) == 0
        f_prev = jnp.where(first, fp_ref[0], pltpu.roll(f, 1, 0))
    else:
        f_prev = fp_ref[...]
    fs = f + (f_prev - f) * mu_ref[...]
    w = RW_WIDTH
    r, k, v = fs[:, 0:w], fs[:, w:2 * w], fs[:, 2 * w:3 * w]
    x3 = fs[:, 3 * w:]
    lw = jnp.dot(jnp.tanh(x3).astype(BF16), w2_ref[...], preferred_element_type=F32)
    la = jnp.dot(x3.astype(BF16), a2_ref[...], preferred_element_type=F32)
    g = jnp.dot(jax.nn.sigmoid(x3).astype(BF16), g2_ref[...], preferred_element_type=F32)
    z = -(w0_ref[...] + lw)
    w_log = -(jnp.maximum(z, 0.0) + jnp.log1p(jnp.exp(-jnp.abs(z)))) - 0.5
    a = jax.nn.sigmoid(a0_ref[...] + la)
    r_o[...] = r
    v_o[...] = v
    kk_o[...] = k * kk_ref[...]
    k_o[...] = k * (1.0 + (a - 1.0) * ka_ref[...])
    a_o[...] = a
    lw_o[...] = -jnp.exp(w_log)
    g_o[...] = g


def _rw_prep(f, prev, params, row0, m, tm, rolled):
    i0 = row0 // tm
    row = lambda width: pl.BlockSpec((tm, width), lambda i: (i, 0))
    par = lambda shape: pl.BlockSpec(shape, lambda i: (0, 0))
    prev_spec = pl.BlockSpec((1, 1, RW_PAD), lambda i: (i, 0, 0)) if rolled else row(RW_PAD)
    return pl.pallas_call(
        functools.partial(_rw_prep_body, rolled=rolled), grid=(m // tm,),
        in_specs=[pl.BlockSpec((tm, RW_PAD), lambda i: (i0 + i, 0)), prev_spec, par((1, RW_PAD))]
                 + [par((1, RW_WIDTH))] * 4 + [par((RW_LORA_PAD, RW_WIDTH))] * 3,
        out_specs=[row(RW_WIDTH)] * 7,
        out_shape=[jax.ShapeDtypeStruct((m, RW_WIDTH), F32)] * 7,
        compiler_params=_cp(("parallel",)), name="rwkv_prep")(f, prev, *params)


def _split(x):
    hi = x.astype(BF16)
    return hi, (x - hi.astype(F32)).astype(BF16)


_NN = ((1,), (0,))
_NT = ((1,), (1,))
_TN = ((0,), (0,))


def _dot3(a, b, dims):
    (ah, al), (bh, bl) = a, b
    f = lambda x, y: lax.dot_general(x, y, (dims, ((), ())), preferred_element_type=F32)
    return f(ah, bh) + (f(ah, bl) + f(al, bh))


def _rw_scan_body(r_ref, k_ref, v_ref, kk_ref, a_ref, lw_ref, g_ref, rk_ref, lnw_ref, lnb_ref, s0_ref,
                  o_ref, st_ref, s_scr, *, heads):
    c = pl.program_id(2)
    cs = r_ref.shape[0]
    n = RW_HEAD

    @pl.when(c == 0)
    def _():
        s_scr[...] = s0_ref[0]

    row = lax.broadcasted_iota(jnp.int32, (cs, cs), 0)
    col = lax.broadcasted_iota(jnp.int32, (cs, cs), 1)
    incl = row >= col
    strict = row > col
    ones_tril = incl.astype(BF16)
    steps = cs.bit_length() - 1
    hs = range(heads)
    sls = [slice(h * n, (h + 1) * n) for h in hs]
    r, k, v, kk, a, lw = ([ref[:, sl] for sl in sls] for ref in (r_ref, k_ref, v_ref, kk_ref, a_ref, lw_ref))
    s0 = [s_scr[h] for h in hs]
    kkn = [kk[h] / jnp.maximum(jnp.sqrt(jnp.sum(kk[h] * kk[h], axis=-1, keepdims=True)), 1e-12) for h in hs]
    lw_hi = [lw[h].astype(BF16) for h in hs]
    lw_mid = [(lw[h] - lw_hi[h].astype(F32)).astype(BF16) for h in hs]
    lw_lo = [(lw[h] - lw_hi[h].astype(F32) - lw_mid[h].astype(F32)).astype(BF16) for h in hs]
    tri = lambda part: jnp.dot(ones_tril, part, preferred_element_type=F32)
    cum = [tri(lw_lo[h]) + tri(lw_mid[h]) + tri(lw_hi[h]) for h in hs]
    cum_end = [cum[h][cs - 1:cs, :] for h in hs]
    e_neg = [jnp.exp(-cum[h]) for h in hs]
    e_end = [jnp.exp(cum_end[h] - cum[h]) for h in hs]
    a_t = [-kkn[h] * jnp.exp(cum[h] - lw[h]) for h in hs]
    r_t = [r[h] * jnp.exp(cum[h]) for h in hs]
    ar = [_split(jnp.concatenate([a_t[h], r_t[h]], axis=0)) for h in hs]
    b_s = [_split(kkn[h] * a[h] * e_neg[h]) for h in hs]
    k_s = [_split(k[h] * e_neg[h]) for h in hs]
    v_s = [_split(v[h]) for h in hs]
    s0_s = [_split(s0[h]) for h in hs]
    ab_rb = [_dot3(ar[h], b_s[h], _NT) for h in hs]
    ak_rk = [_dot3(ar[h], k_s[h], _NT) for h in hs]
    l_ak = [_split(jnp.where(strict, ak_rk[h][:cs], 0.0)) for h in hs]
    mv = [_dot3(l_ak[h], v_s[h], _NN) for h in hs]
    x = [jnp.concatenate([a_t[h], mv[h]], axis=1) for h in hs]
    lp = [jnp.where(strict, ab_rb[h][:cs], 0.0) for h in hs]
    for i in range(steps):
        lp_s = [_split(lp[h]) for h in hs]
        x = [x[h] + _dot3(lp_s[h], _split(x[h]), _NN) for h in hs]
        if i < steps - 1:
            lp = [_dot3(lp_s[h], lp_s[h], _NN) for h in hs]
    u = [_dot3(_split(x[h][:, :n]), s0_s[h], _NT) + x[h][:, n:] for h in hs]
    l_rb = [_split(jnp.where(incl, ab_rb[h][cs:], 0.0)) for h in hs]
    l_rk = [_split(jnp.where(incl, ak_rk[h][cs:], 0.0)) for h in hs]
    y = [_dot3((ar[h][0][cs:], ar[h][1][cs:]), s0_s[h], _NT) + _dot3(l_rk[h], v_s[h], _NN) for h in hs]
    y = [y[h] + _dot3(l_rb[h], _split(u[h]), _NN) for h in hs]
    uv = [_split(jnp.concatenate([u[h], v[h]], axis=0)) for h in hs]
    bk = [_split(jnp.concatenate([kkn[h] * a[h] * e_end[h], k[h] * e_end[h]], axis=0)) for h in hs]
    for h in hs:
        s_scr[h] = s0[h] * jnp.exp(cum_end[h]) + _dot3(uv[h], bk[h], _TN)
    outs = []
    for h in hs:
        mean = jnp.mean(y[h], axis=-1, keepdims=True)
        var = jnp.mean(jnp.square(y[h] - mean), axis=-1, keepdims=True)
        yn = (y[h] - mean) * lax.rsqrt(var + RW_LN_EPS) * lnw_ref[:, sls[h]] + lnb_ref[:, sls[h]]
        bonus = jnp.sum(r[h] * k[h] * rk_ref[:, sls[h]], axis=-1, keepdims=True) * v[h]
        outs.append((yn + bonus) * g_ref[:, sls[h]])
    o_ref[...] = jnp.concatenate(outs, axis=1).astype(o_ref.dtype)

    @pl.when(c == pl.num_programs(2) - 1)
    def _():
        st_ref[0] = s_scr[...]


def _rw_scan(prep, r_k, ln_w, ln_b, state0, row0, nb, t, cs, heads=12):
    nc = t // cs
    blk0 = row0 // cs
    width = heads * RW_HEAD
    seq = pl.BlockSpec((cs, width), lambda p, b, c: (blk0 + b * nc + c, p))
    par = pl.BlockSpec((1, width), lambda p, b, c: (0, p))
    st = pl.BlockSpec((1, heads, RW_HEAD, RW_HEAD), lambda p, b, c: (b, p, 0, 0))
    return pl.pallas_call(
        functools.partial(_rw_scan_body, heads=heads), grid=(RW_HEADS // heads, nb, nc),
        in_specs=[seq] * 7 + [par] * 3 + [st],
        out_specs=[pl.BlockSpec((cs, width), lambda p, b, c: (b * nc + c, p)), st],
        out_shape=[jax.ShapeDtypeStruct((nb * t, RW_WIDTH), BF16),
                   jax.ShapeDtypeStruct((nb, RW_HEADS, RW_HEAD, RW_HEAD), F32)],
        scratch_shapes=[pltpu.VMEM((heads, RW_HEAD, RW_HEAD), F32)],
        compiler_params=_cp(("parallel", "parallel", "arbitrary")), name=f"rwkv_scan_{cs}")(
            *prep, r_k, ln_w, ln_b, state0)


def _s5_body(u_ref, bm_ref, cm_ref, lp_ref, d_ref, x0_ref, y_ref, xt_ref, x_scr, cin_scr, carry_scr, *, exact):
    i = pl.program_id(2)
    tt = u_ref.shape[0]
    nb = tt // SUBLANES
    w = S5_TILE_ST

    @pl.when(i == 0)
    def _():
        carry_scr[...] = x0_ref[0, 0]

    u = u_ref[...]
    if exact:
        bu = jnp.dot(u, bm_ref[0], precision=HIGHEST, preferred_element_type=F32)
    else:
        bu = jnp.dot(u.astype(BF16), bm_ref[0].astype(BF16), preferred_element_type=F32)
    nct = w // LANES
    for ct in range(2 * nct):
        x_scr[ct] = bu[:, ct * LANES:(ct + 1) * LANES]

    def lam_pow(t, ct):
        return (lp_ref[0, t:t + 1, ct * LANES:(ct + 1) * LANES],
                lp_ref[0, t:t + 1, w + ct * LANES:w + (ct + 1) * LANES])

    def cmul_add(xr, xi, cr, ci, sr, si):
        return xr + cr * sr - ci * si, xi + cr * si + ci * sr

    def rows(ct, t):
        return x_scr.at[ct, pl.ds(t, nb, stride=SUBLANES), :]

    for ct in range(nct):
        lam_r, lam_i = lam_pow(0, ct)
        for t in range(1, SUBLANES):
            nr, ni = cmul_add(rows(ct, t)[...], rows(nct + ct, t)[...], lam_r, lam_i,
                              rows(ct, t - 1)[...], rows(nct + ct, t - 1)[...])
            rows(ct, t)[...] = nr
            rows(nct + ct, t)[...] = ni

    lam8 = jnp.concatenate([lp_ref[0, SUBLANES - 1:SUBLANES, :w], lp_ref[0, SUBLANES - 1:SUBLANES, w:]], axis=0)

    def block_carry(b, carry):
        last = b * SUBLANES + SUBLANES - 1
        for ct in range(2 * nct):
            half, lt = divmod(ct, nct)
            cin_scr[ct, pl.ds(b, 1), :] = carry[half:half + 1, lt * LANES:(lt + 1) * LANES]
        end_r = jnp.concatenate([x_scr[ct, pl.ds(last, 1), :] for ct in range(nct)], axis=1)
        end_i = jnp.concatenate([x_scr[nct + ct, pl.ds(last, 1), :] for ct in range(nct)], axis=1)
        nr, ni = cmul_add(end_r, end_i, lam8[0:1], lam8[1:2], carry[0:1], carry[1:2])
        return jnp.concatenate([nr, ni], axis=0)

    carry0 = jnp.concatenate([carry_scr[:, :w], carry_scr[:, w:]], axis=0)
    carry = lax.fori_loop(0, nb, block_carry, carry0)
    carry_scr[...] = jnp.concatenate([carry[0:1], carry[1:2]], axis=1)
    for ct in range(nct):
        cin_r, cin_i = cin_scr[ct], cin_scr[nct + ct]
        for t in range(SUBLANES):
            pr, pi_ = lam_pow(t, ct)
            nr, ni = cmul_add(rows(ct, t)[...], rows(nct + ct, t)[...], pr, pi_, cin_r, cin_i)
            rows(ct, t)[...] = nr
            rows(nct + ct, t)[...] = ni

    xs = jnp.concatenate([x_scr[ct] for ct in range(2 * nct)], axis=1)
    if exact:
        y = jnp.dot(xs, cm_ref[0], precision=HIGHEST, preferred_element_type=F32)
    else:
        y = jnp.dot(xs.astype(BF16), cm_ref[0].astype(BF16), preferred_element_type=F32)
    y_ref[...] = y + d_ref[...] * u

    @pl.when(i == pl.num_programs(2) - 1)
    def _():
        xt_ref[0, 0] = carry_scr[...]


def _s5_scan(u, bm, cm, lp, d, x0, row0, nb_seq, t, tt, exact):
    nt = t // tt
    i0 = row0 // tt
    w2 = 2 * S5_TILE_ST
    return pl.pallas_call(
        functools.partial(_s5_body, exact=exact), grid=(S5_TILES, nb_seq, nt),
        in_specs=[pl.BlockSpec((tt, S5_TILE_IN), lambda j, b, i: (i0 + b * nt + i, j)),
                  pl.BlockSpec((1, S5_TILE_IN, w2), lambda j, b, i: (j, 0, 0)),
                  pl.BlockSpec((1, w2, S5_TILE_IN), lambda j, b, i: (j, 0, 0)),
                  pl.BlockSpec((1, SUBLANES, w2), lambda j, b, i: (j, 0, 0)),
                  pl.BlockSpec((1, S5_TILE_IN), lambda j, b, i: (0, j)),
                  pl.BlockSpec((1, 1, 1, w2), lambda j, b, i: (b, j, 0, 0))],
        out_specs=[pl.BlockSpec((tt, S5_TILE_IN), lambda j, b, i: (b * nt + i, j)),
                   pl.BlockSpec((1, 1, 1, w2), lambda j, b, i: (b, j, 0, 0))],
        out_shape=[jax.ShapeDtypeStruct((nb_seq * t, S5_WIDTH), F32),
                   jax.ShapeDtypeStruct((nb_seq, S5_TILES, 1, w2), F32)],
        scratch_shapes=[pltpu.VMEM((w2 // LANES, tt, LANES), F32), pltpu.VMEM((w2 // LANES, tt // SUBLANES, LANES), F32),
                        pltpu.VMEM((1, w2), F32)],
        compiler_params=_cp(("parallel", "parallel", "arbitrary")), name=f"s5_scan_{tt}")(u, bm, cm, lp, d, x0)


def _gelu_tanh(y):
    return 0.5 * y * (1.0 + jnp.tanh(math.sqrt(2.0 / math.pi) * (y + 0.044715 * (y * y * y))))


def _s5_glu_body(y_ref, w_ref, b_ref, o_ref, *, tn):
    j = pl.program_id(1)
    z = _gelu_tanh(y_ref[...])
    gate = jax.nn.sigmoid(jnp.dot(z.astype(BF16), w_ref[...], preferred_element_type=F32) + b_ref[...])
    zj = _gelu_tanh(y_ref[:, pl.ds(pl.multiple_of(j * tn, tn), tn)])
    o_ref[...] = (zj * gate).astype(o_ref.dtype)


def _s5_glu(y, w, b, tn=512):
    m, d = y.shape
    tm = _pick(m, (256, 128, 16, 8))
    return pl.pallas_call(
        functools.partial(_s5_glu_body, tn=tn), grid=(m // tm, d // tn),
        in_specs=[pl.BlockSpec((tm, d), lambda i, j: (i, 0)), pl.BlockSpec((d, tn), lambda i, j: (0, j)),
                  pl.BlockSpec((1, tn), lambda i, j: (0, j))],
        out_specs=pl.BlockSpec((tm, tn), lambda i, j: (i, j)),
        out_shape=jax.ShapeDtypeStruct((m, d), BF16),
        compiler_params=_cp(("parallel", "parallel")), name="s5_glu")(y, w, b)


def _merge_body(oa_ref, ob_ref, oc_ref, ga_ref, gb_ref, gc_ref, wa_ref, wb_ref, wc_ref, o_ref):
    acc = ga_ref[...] * jnp.dot(oa_ref[...], wa_ref[...], preferred_element_type=F32)
    acc = acc + gb_ref[...] * jnp.dot(ob_ref[...], wb_ref[...], preferred_element_type=F32)
    acc = acc + gc_ref[...] * jnp.dot(oc_ref[...], wc_ref[...], preferred_element_type=F32)
    o_ref[...] = acc.astype(o_ref.dtype)


def _merge(o_a, o_b, o_c, gates, w_a, w_b, w_c, tn=512):
    m = o_a.shape[0]
    tm = _row_tile(m)
    nj = D_MODEL // tn
    full = lambda width: pl.BlockSpec((tm, width), lambda i, j: (i, 0))
    gate = lambda which: pl.BlockSpec((tm, tn), lambda i, j: (i, which * nj + j))
    wcol = lambda rows: pl.BlockSpec((rows, tn), lambda i, j: (0, j))
    return pl.pallas_call(
        _merge_body, grid=(m // tm, nj),
        in_specs=[full(A_WIDTH), full(RW_WIDTH), full(S5_WIDTH), gate(0), gate(1), gate(2),
                  wcol(A_WIDTH), wcol(RW_WIDTH), wcol(S5_WIDTH)],
        out_specs=pl.BlockSpec((tm, tn), lambda i, j: (i, j)),
        out_shape=jax.ShapeDtypeStruct((m, D_MODEL), BF16),
        compiler_params=_cp(("parallel", "parallel")), name="merge")(o_a, o_b, o_c, gates, gates, gates, w_a, w_b, w_c)


def _expert_body(te_ref, nu_ref, tok_ref, x_hbm, wg_ref, wu_ref, wd_ref, o_ref, xrows, xb, hg_acc, hu_acc, hb, sem):
    i, j = pl.program_id(0), pl.program_id(1)
    nu = nu_ref[0]
    used = i < nu

    def row_copy(tile, r):
        return pltpu.make_async_copy(x_hbm.at[pl.ds(tok_ref[tile * MOE_TM + r], 1)], xrows.at[pl.ds(r, 1)], sem.at[0])

    def start_gather(tile):
        def body(r, c):
            row_copy(tile, r).start()
            return c
        lax.fori_loop(0, MOE_TM, body, 0)

    def wait_gather(tile):
        def body(r, c):
            row_copy(tile, r).wait()
            return c
        lax.fori_loop(0, MOE_TM, body, 0)

    @pl.when(jnp.logical_and(j == 0, jnp.logical_and(i == 0, nu > 0)))
    def _():
        start_gather(0)

    @pl.when(jnp.logical_and(j == 0, used))
    def _():
        wait_gather(i)
        xb[...] = xrows[...].astype(BF16)

    @pl.when(jnp.logical_and(j == 0, i + 1 < nu))
    def _():
        start_gather(i + 1)

    down = j >= MOE_KSTEPS

    @pl.when(jnp.logical_and(used, jnp.logical_not(down)))
    def _():
        x = xb[:, pl.ds(pl.multiple_of(j * MOE_TK, MOE_TK), MOE_TK)]
        pg = jnp.dot(x, wg_ref[...].astype(BF16), preferred_element_type=F32)
        pu = jnp.dot(x, wu_ref[...].astype(BF16), preferred_element_type=F32)

        @pl.when(j == 0)
        def _():
            hg_acc[...] = pg
            hu_acc[...] = pu

        @pl.when(j > 0)
        def _():
            hg_acc[...] += pg
            hu_acc[...] += pu

    @pl.when(jnp.logical_and(used, j == MOE_KSTEPS))
    def _():
        hg = hg_acc[...]
        hb[...] = (hg * jax.nn.sigmoid(hg) * hu_acc[...]).astype(BF16)

    @pl.when(jnp.logical_and(used, down))
    def _():
        o_ref[...] = jnp.dot(hb[...], wd_ref[...].astype(BF16), preferred_element_type=F32)

    @pl.when(jnp.logical_and(jnp.logical_not(used), down))
    def _():
        o_ref[...] = jnp.zeros_like(o_ref)


def _experts(x, slot_tok, tile_expert, n_used, layer, w_gate, w_up, w_down):
    d = x.shape[1]
    n_tiles = slot_tok.shape[0] // MOE_TM
    n_down = d // MOE_TN

    def wmap(i, j, te, nu, tok):
        live = i < nu[0]
        return layer, te[jnp.minimum(i, nu[0] - 1)], jnp.where(live, jnp.minimum(j, MOE_KSTEPS - 1), MOE_KSTEPS - 1), 0

    def wdmap(i, j, te, nu, tok):
        live = i < nu[0]
        return layer, te[jnp.minimum(i, nu[0] - 1)], 0, jnp.where(live, jnp.maximum(j - MOE_KSTEPS, 0), n_down - 1)

    grid_spec = pltpu.PrefetchScalarGridSpec(
        num_scalar_prefetch=3, grid=(n_tiles, MOE_KSTEPS + n_down),
        in_specs=[pl.BlockSpec(memory_space=pl.ANY),
                  pl.BlockSpec((None, None, MOE_TK, D_EXPERT), wmap), pl.BlockSpec((None, None, MOE_TK, D_EXPERT), wmap),
                  pl.BlockSpec((None, None, D_EXPERT, MOE_TN), wdmap)],
        out_specs=pl.BlockSpec((MOE_TM, MOE_TN), lambda i, j, te, nu, tok: (i, jnp.maximum(j - MOE_KSTEPS, 0))),
        scratch_shapes=[pltpu.VMEM((MOE_TM, d), F32), pltpu.VMEM((MOE_TM, d), BF16),
                        pltpu.VMEM((MOE_TM, D_EXPERT), F32), pltpu.VMEM((MOE_TM, D_EXPERT), F32),
                        pltpu.VMEM((MOE_TM, D_EXPERT), BF16), pltpu.SemaphoreType.DMA((1,))])
    return pl.pallas_call(
        _expert_body, grid_spec=grid_spec, out_shape=jax.ShapeDtypeStruct((n_tiles * MOE_TM, d), F32),
        compiler_params=_cp(("arbitrary", "arbitrary"), 56), name="experts")(
            tile_expert, n_used, slot_tok, x, w_gate, w_up, w_down)


def _combine_body(slot_ref, x_ref, g_ref, out_hbm, o_ref, buf, sem, *, tc):
    i = pl.program_id(0)
    b = i % 2

    def row_copy(tile, bb, r):
        return pltpu.make_async_copy(out_hbm.at[pl.ds(slot_ref[tile * 2 * tc + r], 1)], buf.at[bb, pl.ds(r, 1)],
                                     sem.at[bb])

    def start_gather(tile, bb):
        def body(r, c):
            row_copy(tile, bb, r).start()
            return c
        lax.fori_loop(0, 2 * tc, body, 0)

    @pl.when(i == 0)
    def _():
        start_gather(0, 0)

    @pl.when(i + 1 < pl.num_programs(0))
    def _():
        start_gather(i + 1, 1 - b)

    def wait_body(r, c):
        row_copy(i, b, r).wait()
        return c
    lax.fori_loop(0, 2 * tc, wait_body, 0)
    rows = buf[b]
    g = g_ref[...]
    o_ref[...] = x_ref[...] + (g[:, 0:1] * rows[:tc] + g[:, 1:2] * rows[tc:])


def _combine(x, gate, slot, out):
    n, d = x.shape
    tc = _pick(n, (128, 96, 64, 32, 16, 8))
    tiles = n // tc
    slot_flat = slot.reshape(tiles, tc, TOP_K).transpose(0, 2, 1).reshape(-1)
    grid_spec = pltpu.PrefetchScalarGridSpec(
        num_scalar_prefetch=1, grid=(tiles,),
        in_specs=[pl.BlockSpec((tc, d), lambda i, s: (i, 0)), pl.BlockSpec((tc, TOP_K), lambda i, s: (i, 0)),
                  pl.BlockSpec(memory_space=pl.ANY)],
        out_specs=pl.BlockSpec((tc, d), lambda i, s: (i, 0)),
        scratch_shapes=[pltpu.VMEM((2, TOP_K * tc, d), F32), pltpu.SemaphoreType.DMA((2,))])
    return pl.pallas_call(
        functools.partial(_combine_body, tc=tc), grid_spec=grid_spec, out_shape=jax.ShapeDtypeStruct((n, d), F32),
        compiler_params=_cp(("arbitrary",)), name="moe_combine")(slot_flat, x, gate, out)


def _moe(x, xn, rw, rg_b, re_b, layer, w_gate, w_up, w_down):
    n, d = xn.shape
    logits = _matmul(xn, rw, tn=LANES, name="router")
    g_logits = logits[:, :N_GROUPS] + rg_b
    grp = jnp.argmax(g_logits, axis=-1).astype(jnp.int32)
    p_grp = jnp.take_along_axis(jax.nn.softmax(g_logits, axis=-1), grp[:, None], axis=-1)
    e_logits = (logits[:, N_GROUPS:N_GROUPS + N_EXPERTS] + re_b).reshape(n, N_GROUPS, EXPERTS_PER_GROUP)
    e_logits = jnp.take_along_axis(e_logits, grp[:, None, None], axis=1)[:, 0]
    top_v, top_i = lax.top_k(e_logits, TOP_K)
    gate = jax.nn.softmax(top_v, axis=-1) * p_grp
    expert_id = (grp[:, None] * EXPERTS_PER_GROUP + top_i).astype(jnp.int32).reshape(-1)

    a = n * TOP_K
    n_tiles = -(-(a + N_EXPERTS * (MOE_TM - 1)) // MOE_TM)
    idx = jnp.arange(a, dtype=jnp.int32)
    order = jnp.argsort(expert_id * a + idx)
    e_sorted = expert_id[order]
    counts = jnp.bincount(expert_id, length=N_EXPERTS).astype(jnp.int32)
    starts = jnp.cumsum(counts) - counts
    padded = (counts + MOE_TM - 1) // MOE_TM * MOE_TM
    pad_end = jnp.cumsum(padded)
    pad_start = pad_end - padded
    slot_sorted = (pad_start[e_sorted] + idx - starts[e_sorted]).astype(jnp.int32)
    slot = jnp.zeros((a,), jnp.int32).at[order].set(slot_sorted)
    slot_tok = jnp.zeros((n_tiles * MOE_TM,), jnp.int32).at[slot].set(idx // TOP_K)
    tile_expert = jnp.minimum(
        jnp.searchsorted(pad_end, jnp.arange(n_tiles, dtype=jnp.int32) * MOE_TM, side='right'),
        N_EXPERTS - 1).astype(jnp.int32)
    n_used = (pad_end[-1:] // MOE_TM).astype(jnp.int32)
    out = _experts(xn, slot_tok, tile_expert, n_used, layer, w_gate, w_up, w_down)
    return _combine(x, gate, slot.reshape(n, TOP_K), out)


def _s5_params(lam_re, lam_im, b_re, b_im, c_re, c_im, log_step):
    dt = jnp.exp(log_step)[:, None]
    mag = jnp.exp(lam_re * dt)
    lbr, lbi = mag * jnp.cos(lam_im * dt), mag * jnp.sin(lam_im * dt)
    den = lam_re * lam_re + lam_im * lam_im
    qr = ((lbr - 1.0) * lam_re + lbi * lam_im) / den
    qi = (lbi * lam_re - (lbr - 1.0) * lam_im) / den
    bbr = qr[:, :, None] * b_re - qi[:, :, None] * b_im
    bbi = qr[:, :, None] * b_im + qi[:, :, None] * b_re
    eye = jnp.eye(S5_TILE_GROUPS, dtype=F32)

    def blockdiag_in(m):
        m = m.reshape(S5_TILES, S5_TILE_GROUPS, S5_STATE, S5_GROUP)
        return jnp.einsum('jgph,ge->jghep', m, eye).reshape(S5_TILES, S5_TILE_IN, S5_TILE_ST)

    def blockdiag_out(m):
        m = m.reshape(S5_TILES, S5_TILE_GROUPS, S5_GROUP, S5_STATE)
        return jnp.einsum('jghp,ge->jgpeh', m, eye).reshape(S5_TILES, S5_TILE_ST, S5_TILE_IN)

    bm = jnp.concatenate([blockdiag_in(bbr), blockdiag_in(bbi)], axis=2)
    cm = jnp.concatenate([blockdiag_out(c_re), blockdiag_out(-c_im)], axis=1)
    pr, pi_ = [lbr], [lbi]
    for _ in range(SUBLANES - 1):
        pr, pi_ = pr + [pr[-1] * lbr - pi_[-1] * lbi], pi_ + [pr[-1] * lbi + pi_[-1] * lbr]
    tile = lambda p: jnp.stack(p, 0).reshape(SUBLANES, S5_TILES, S5_TILE_ST).transpose(1, 0, 2)
    lp = jnp.concatenate([tile(pr), tile(pi_)], axis=2)
    return bm, cm, lp


def _s5_state_in(re, im):
    b = re.shape[0]
    f = lambda x: x.reshape(b, S5_TILES, 1, S5_TILE_ST)
    return jnp.concatenate([f(re), f(im)], axis=3)


def _s5_state_out(x):
    b = x.shape[0]
    return (x[..., 0, :S5_TILE_ST].reshape(b, S5_GROUPS, S5_STATE), x[..., 0, S5_TILE_ST:].reshape(b, S5_GROUPS, S5_STATE))


def _pad_rows(w, rows, at):
    return jnp.zeros((rows, w.shape[1]), w.dtype).at[at:at + w.shape[0]].set(w)


def _layer(x, tp, nbs, ts, rope_cos, rope_sin, cache_k, cache_v, shift0, rw0, s5re0, s5im0, lw, stacked):
    n = x.shape[0]
    bf = lambda w: w.astype(BF16)
    layer, w_in_all = stacked[0], stacked[1]
    experts = (layer,) + tuple(stacked[2:])
    w_pk = _repack_w_in(w_in_all, layer)
    pk_s5 = OFF_V + RW_PAD
    xn = _rmsnorm(x, lw['norm_mix_g'], BF16)
    qkv = _matmul(xn, w_pk, col0=0, n=OFF_V, tn=256, name="proj_qkv")
    f = _matmul(xn, w_pk, col0=OFF_V, n=RW_PAD, tn=256, name="proj_rwkv")
    u = _matmul(xn, w_pk, col0=pk_s5, n=S5_WIDTH, name="proj_s5")
    gates = _matmul(xn, w_pk, col0=pk_s5 + S5_WIDTH, n=3 * D_MODEL, act="sigmoid", name="proj_gates")

    q_rot, k_rot = _rope(qkv, rope_cos, rope_sin)
    o_ap = _attn_prompt(q_rot, k_rot, qkv, lw['attn_sinks'], tp)
    ck = cache_k.reshape(nbs, WINDOW, A_KV_WIDTH)
    cv = cache_v.reshape(nbs, WINDOW, A_KV_WIDTH)
    o_as, nk_s, nv_s = _attn_sample(q_rot, k_rot, qkv, ck, cv, lw['attn_sinks'], tp, nbs, ts)
    o_a = jnp.concatenate([o_ap, o_as], axis=0)
    kv_shape = (-1, WINDOW, A_KV_HEADS, HEAD_DIM)
    nk_p = k_rot[tp - WINDOW:tp].reshape(kv_shape)
    nv_p = qkv[tp - WINDOW:tp, OFF_K:OFF_V].reshape(kv_shape)

    ms = nbs * ts
    tm_p = _pick(tp, (256, 128, 64))
    assert tp % ms == 0, "sample rows must start on a tile boundary"
    above = jnp.concatenate([jnp.zeros((1, RW_PAD), F32), f[tm_p - 1:tp - 1:tm_p]], axis=0)[:, None, :]
    f_s = f[tp:].reshape(nbs, ts, RW_PAD)
    sh = jnp.pad(shift0, ((0, 0), (0, RW_PAD - SHIFT_WIDTH)))
    fp_s = jnp.concatenate([sh[:, None, :], f_s[:, :-1]], axis=1).reshape(ms, RW_PAD)
    row = lambda v: v.reshape(1, -1)
    prep_par = (jnp.pad(row(lw['rwkv_mu']), ((0, 0), (0, RW_PAD - SHIFT_WIDTH))), row(lw['rwkv_w0']),
                row(lw['rwkv_a0']), row(lw['rwkv_k_k']), row(lw['rwkv_k_a']),
                bf(_pad_rows(lw['rwkv_w2'], RW_LORA_PAD, 0)), bf(_pad_rows(lw['rwkv_a2'], RW_LORA_PAD, RW_LORA)),
                bf(_pad_rows(lw['rwkv_g2'], RW_LORA_PAD, 2 * RW_LORA)))
    prep_p = _rw_prep(f, above, prep_par, 0, tp, tm_p, True)
    prep_s = _rw_prep(f, fp_s, prep_par, tp, ms, ms, False)
    rw_par = (row(lw['rwkv_r_k']), row(lw['rwkv_ln_w']), row(lw['rwkv_ln_b']))
    zero_rw = jnp.zeros((1, RW_HEADS, RW_HEAD, RW_HEAD), F32)
    o_bp, rw_p = _rw_scan(prep_p, *rw_par, zero_rw, 0, 1, tp, CHUNK)
    o_bs, rw_s = _rw_scan(prep_s, *rw_par, rw0, 0, nbs, ts, ts)
    o_b = jnp.concatenate([o_bp, o_bs], axis=0)
    shift_p = f[tp - 1:tp, :SHIFT_WIDTH]
    shift_s = f_s[:, -1, :SHIFT_WIDTH]

    bm, cm, lp = _s5_params(lw['s5_lambda_re'], lw['s5_lambda_im'], lw['s5_b_re'], lw['s5_b_im'],
                            lw['s5_c_re'], lw['s5_c_im'], lw['s5_log_step'])
    d_row = row(lw['s5_d'])
    zero_s5 = jnp.zeros((1, S5_TILES, 1, 2 * S5_TILE_ST), F32)
    y_p, xs_p = _s5_scan(u, bm, cm, lp, d_row, zero_s5, 0, 1, tp, tm_p, False)
    y_s, xs_s = _s5_scan(u, bm, cm, lp, d_row, _s5_state_in(s5re0, s5im0), tp, nbs, ts, ts, True)
    o_c = _s5_glu(jnp.concatenate([y_p, y_s], axis=0), bf(lw['s5_glu_w']), row(lw['s5_glu_b']))
    s5re_p, s5im_p = _s5_state_out(xs_p)
    s5re_s, s5im_s = _s5_state_out(xs_s)

    merged = _merge(o_a, o_b, o_c, gates, bf(lw['w_branch_attn']), bf(lw['w_branch_rwkv']), bf(lw['w_branch_s5']))
    x = _matmul_residual(merged, bf(lw['w_out']), x)

    xn2 = _rmsnorm(x, lw['norm_ffn_g'], F32)
    rw = jnp.concatenate([lw['router_group_w'], lw['router_expert_w'],
                          jnp.zeros((D_MODEL, LANES - N_GROUPS - N_EXPERTS), F32)], axis=1)
    x = _moe(x, xn2, bf(rw), lw['router_group_b'], lw['router_expert_b'], *experts)
    st_p = (nk_p, nv_p, shift_p, rw_p, s5re_p, s5im_p)
    st_s = (nk_s.reshape(kv_shape), nv_s.reshape(kv_shape), shift_s, rw_s, s5re_s, s5im_s)
    return x, st_p, st_s


def _rope_tables(pos):
    half = HEAD_DIM // 2
    inv_freq = ROPE_THETA ** (-jnp.arange(half, dtype=F32) / half)
    ang = pos.astype(F32)[:, None] * inv_freq[None, :]
    cos, sin = jnp.cos(ang), jnp.sin(ang)
    reps = LANES // HEAD_DIM
    return jnp.tile(jnp.concatenate([cos, cos], axis=1), (1, reps)), jnp.tile(jnp.concatenate([-sin, sin], axis=1), (1, reps))


_LAYER_KEYS = ('norm_mix_g', 'w_in', 'attn_sinks', 'rwkv_mu', 'rwkv_w0', 'rwkv_w2', 'rwkv_a0', 'rwkv_a2', 'rwkv_g2',
               'rwkv_k_k', 'rwkv_k_a', 'rwkv_r_k', 'rwkv_ln_w', 'rwkv_ln_b', 's5_lambda_re', 's5_lambda_im',
               's5_b_re', 's5_b_im', 's5_c_re', 's5_c_im', 's5_d', 's5_log_step', 's5_glu_w', 's5_glu_b',
               'w_branch_attn', 'w_branch_rwkv', 'w_branch_s5', 'w_out', 'norm_ffn_g', 'router_group_w',
               'router_group_b', 'router_expert_w', 'router_expert_b', 'expert_w_gate', 'expert_w_up', 'expert_w_down')


def _forward(x_prompt, x_sample, cache_k, cache_v, state_shift, state_rwkv, state_s5_re, state_s5_im,
             layer_weights, norm_final_g):
    bp, tp, d = x_prompt.shape
    assert bp == 1, "the prompt batch is one new stream"
    nbs, ts, _ = x_sample.shape
    depth = cache_k.shape[0]
    x = jnp.concatenate([x_prompt.reshape(tp, d), x_sample.reshape(nbs * ts, d)], axis=0)
    pos = jnp.concatenate([jnp.arange(tp, dtype=jnp.int32),
                           jnp.tile(PAST_LEN + jnp.arange(ts, dtype=jnp.int32), nbs)])
    rope_cos, rope_sin = _rope_tables(pos)
    new_p = [[] for _ in range(6)]
    new_s = [[] for _ in range(6)]
    for l in range(depth):
        big = ('w_in', 'expert_w_gate', 'expert_w_up', 'expert_w_down')
        lw = {k: v[l] for k, v in zip(_LAYER_KEYS, layer_weights) if k not in big}
        stacked = (l,) + tuple(layer_weights[_LAYER_KEYS.index(k)] for k in big)
        x, st_p, st_s = _layer(x, tp, nbs, ts, rope_cos, rope_sin, cache_k[l], cache_v[l], state_shift[l],
                               state_rwkv[l], state_s5_re[l], state_s5_im[l], lw, stacked)
        for i in range(6):
            new_p[i].append(st_p[i])
            new_s[i].append(st_s[i])
    y = _rmsnorm(x, norm_final_g, F32)
    outs_p = tuple(jnp.stack(t, axis=0) for t in new_p)
    outs_s = tuple(jnp.stack(t, axis=0) for t in new_s)
    return (y[:tp].reshape(1, tp, d), y[tp:].reshape(nbs, ts, d)) + outs_p + outs_s


def kernel(x_prompt, x_sample, cache_k, cache_v, state_shift, state_rwkv, state_s5_re, state_s5_im,
           norm_mix_g, w_in, attn_sinks, rwkv_mu, rwkv_w0, rwkv_w2, rwkv_a0, rwkv_a2, rwkv_g2,
           rwkv_k_k, rwkv_k_a, rwkv_r_k, rwkv_ln_w, rwkv_ln_b, s5_lambda_re, s5_lambda_im,
           s5_b_re, s5_b_im, s5_c_re, s5_c_im, s5_d, s5_log_step, s5_glu_w, s5_glu_b,
           w_branch_attn, w_branch_rwkv, w_branch_s5, w_out, norm_ffn_g, router_group_w,
           router_group_b, router_expert_w, router_expert_b, expert_w_gate, expert_w_up,
           expert_w_down, norm_final_g):
    layer_weights = (norm_mix_g, w_in, attn_sinks, rwkv_mu, rwkv_w0, rwkv_w2, rwkv_a0, rwkv_a2, rwkv_g2,
                     rwkv_k_k, rwkv_k_a, rwkv_r_k, rwkv_ln_w, rwkv_ln_b, s5_lambda_re, s5_lambda_im,
                     s5_b_re, s5_b_im, s5_c_re, s5_c_im, s5_d, s5_log_step, s5_glu_w, s5_glu_b,
                     w_branch_attn, w_branch_rwkv, w_branch_s5, w_out, norm_ffn_g, router_group_w,
                     router_group_b, router_expert_w, router_expert_b, expert_w_gate, expert_w_up, expert_w_down)
    return _forward(x_prompt, x_sample, cache_k, cache_v, state_shift, state_rwkv, state_s5_re, state_s5_im,
                    layer_weights, norm_final_g)
```

```python
import functools
import math

import jax
import jax.numpy as jnp
from jax import lax
from jax.experimental import pallas as pl
from jax.experimental.pallas import tpu as pltpu

F32 = jnp.float32
BF16 = jnp.bfloat16
HIGHEST = lax.Precision.HIGHEST

D_MODEL = 4096
CHUNK = 64
NORM_EPS = 1e-5
A_HEADS = 16
A_KV_HEADS = 2
HEAD_DIM = 64
A_WIDTH = A_HEADS * HEAD_DIM
A_KV_WIDTH = A_KV_HEADS * HEAD_DIM
WINDOW = 128
ROPE_THETA = 10000.0
ATTN_SCALE = HEAD_DIM ** -0.5
PAST_LEN = 4096
RW_HEAD = 64
RW_HEADS = 24
RW_WIDTH = RW_HEADS * RW_HEAD
RW_LORA = 64
RW_LN_EPS = 64e-5
SHIFT_WIDTH = 3 * RW_WIDTH + 3 * RW_LORA
S5_GROUPS = 96
S5_GROUP = 16
S5_WIDTH = S5_GROUPS * S5_GROUP
S5_STATE = 64
OFF_Q = A_WIDTH
OFF_K = OFF_Q + A_KV_WIDTH
OFF_V = OFF_K + A_KV_WIDTH
OFF_RW = OFF_V + SHIFT_WIDTH
OFF_S5 = OFF_RW + S5_WIDTH
OFF_GA = OFF_S5 + D_MODEL
OFF_GB = OFF_GA + D_MODEL
IN_COLS = OFF_GB + D_MODEL
N_GROUPS = 4
EXPERTS_PER_GROUP = 8
N_EXPERTS = N_GROUPS * EXPERTS_PER_GROUP
TOP_K = 2
D_EXPERT = 1024

LANES = 128
SUBLANES = 8
RW_PAD = 4864
RW_LORA_PAD = RW_PAD - 3 * RW_WIDTH
S5_TILE_GROUPS = 8
S5_TILE_IN = S5_TILE_GROUPS * S5_GROUP
S5_TILE_ST = S5_TILE_GROUPS * S5_STATE
S5_TILES = S5_GROUPS // S5_TILE_GROUPS
MOE_TM = 640
MOE_TK = 512
MOE_KSTEPS = D_MODEL // MOE_TK
MOE_TN = 512


def _cp(sem, vmem_mb=48):
    return pltpu.CompilerParams(dimension_semantics=sem, vmem_limit_bytes=vmem_mb * 2 ** 20)


def _pick(n, prefs):
    for p in prefs:
        if n % p == 0:
            return p
    raise ValueError(f"no tile for {n} in {prefs}")


def _rmsnorm_body(x_ref, g_ref, o_ref):
    x = x_ref[...]
    inv = lax.rsqrt(jnp.mean(x * x, axis=-1, keepdims=True) + NORM_EPS)
    o_ref[...] = (x * inv * g_ref[...]).astype(o_ref.dtype)


def _rmsnorm(x, g, out_dtype):
    m, d = x.shape
    tm = _pick(m, (256, 128, 16, 8))
    return pl.pallas_call(
        _rmsnorm_body, grid=(m // tm,),
        in_specs=[pl.BlockSpec((tm, d), lambda i: (i, 0)), pl.BlockSpec((1, d), lambda i: (0, 0))],
        out_specs=pl.BlockSpec((tm, d), lambda i: (i, 0)),
        out_shape=jax.ShapeDtypeStruct((m, d), out_dtype),
        compiler_params=_cp(("parallel",)), name="rmsnorm")(x, g.reshape(1, d))


def _mm_body(a_ref, b_ref, o_ref, *, act):
    acc = jnp.dot(a_ref[...].astype(BF16), b_ref[...], preferred_element_type=F32)
    if act == "sigmoid":
        acc = jax.nn.sigmoid(acc)
    o_ref[...] = acc.astype(o_ref.dtype)


def _mm_res_body(a_ref, b_ref, r_ref, o_ref):
    o_ref[...] = r_ref[...] + jnp.dot(a_ref[...], b_ref[...], preferred_element_type=F32)


def _row_tile(m):
    return _pick(m, (768, 512, 256, 128, 64, 16, 8))


def _matmul(a, b, *, col0=0, n=None, act=None, out_dtype=F32, tn=512, name="matmul"):
    m, k = a.shape
    n = b.shape[1] if n is None else n
    tm = _row_tile(m)
    tn = _pick(math.gcd(n, col0) if col0 else n, (tn, 256, 128))
    j0 = col0 // tn
    return pl.pallas_call(
        functools.partial(_mm_body, act=act), grid=(m // tm, n // tn),
        in_specs=[pl.BlockSpec((tm, k), lambda i, j: (i, 0)), pl.BlockSpec((k, tn), lambda i, j: (0, j0 + j))],
        out_specs=pl.BlockSpec((tm, tn), lambda i, j: (i, j)),
        out_shape=jax.ShapeDtypeStruct((m, n), out_dtype),
        compiler_params=_cp(("parallel", "parallel")), name=name)(a, b)


def _matmul_residual(a, b, res, *, tn=512):
    m, k = a.shape
    n = b.shape[1]
    tm = _row_tile(m)
    return pl.pallas_call(
        _mm_res_body, grid=(m // tm, n // tn),
        in_specs=[pl.BlockSpec((tm, k), lambda i, j: (i, 0)), pl.BlockSpec((k, tn), lambda i, j: (0, j)),
                  pl.BlockSpec((tm, tn), lambda i, j: (i, j))],
        out_specs=pl.BlockSpec((tm, tn), lambda i, j: (i, j)),
        out_shape=jax.ShapeDtypeStruct((m, n), F32),
        compiler_params=_cp(("parallel", "parallel")), name="matmul_residual")(a, b, res)


def _repack_body(w_ref, o_ref):
    w = w_ref[...].astype(F32)
    pad = jnp.zeros((w.shape[0], RW_PAD - SHIFT_WIDTH), F32)
    o_ref[...] = jnp.concatenate([w[:, :OFF_RW], pad, w[:, OFF_RW:]], axis=1).astype(o_ref.dtype)


def _repack_w_in(w_in_all, layer):
    _, k, n = w_in_all.shape
    tr = 64
    n_out = n + RW_PAD - SHIFT_WIDTH
    return pl.pallas_call(
        _repack_body, grid=(k // tr,),
        in_specs=[pl.BlockSpec((None, tr, n), lambda i: (layer, i, 0))],
        out_specs=pl.BlockSpec((tr, n_out), lambda i: (i, 0)),
        out_shape=jax.ShapeDtypeStruct((k, n_out), BF16),
        compiler_params=_cp(("parallel",)), name="repack_w_in")(w_in_all)


def _rope_body(x_ref, cos_ref, sin_ref, q_ref, k_ref):
    cos = cos_ref[...]
    sin = sin_ref[...]
    lane = lax.broadcasted_iota(jnp.int32, cos.shape, 1)
    first = (lane % HEAD_DIM) < (HEAD_DIM // 2)
    for j in range((A_WIDTH + A_KV_WIDTH) // LANES):
        x = x_ref[:, j * LANES:(j + 1) * LANES]
        partner = jnp.where(first, pltpu.roll(x, LANES - HEAD_DIM // 2, 1), pltpu.roll(x, HEAD_DIM // 2, 1))
        y = x * cos + partner * sin
        if j < A_WIDTH // LANES:
            q_ref[:, j * LANES:(j + 1) * LANES] = y
        else:
            k_ref[...] = y


def _rope(qkv, cos, sin):
    m, w = qkv.shape
    tm = _pick(m, (256, 128, 16, 8))
    return pl.pallas_call(
        _rope_body, grid=(m // tm,),
        in_specs=[pl.BlockSpec((tm, w), lambda i: (i, 0)), pl.BlockSpec((tm, LANES), lambda i: (i, 0)),
                  pl.BlockSpec((tm, LANES), lambda i: (i, 0))],
        out_specs=[pl.BlockSpec((tm, A_WIDTH), lambda i: (i, 0)), pl.BlockSpec((tm, LANES), lambda i: (i, 0))],
        out_shape=[jax.ShapeDtypeStruct((m, A_WIDTH), F32), jax.ShapeDtypeStruct((m, LANES), F32)],
        compiler_params=_cp(("parallel",)), name="rope")(qkv, cos, sin)


def _attend(q, k_all, v_all, sink_ref, valid, o_ref):
    rows = q.shape[0]
    grp = A_HEADS // A_KV_HEADS
    outs = []
    for h in range(A_KV_HEADS):
        kh = k_all[:, h * HEAD_DIM:(h + 1) * HEAD_DIM]
        vh = v_all[:, h * HEAD_DIM:(h + 1) * HEAD_DIM]
        qs = jnp.concatenate(
            [q[:, (h * grp + g) * HEAD_DIM:(h * grp + g + 1) * HEAD_DIM] for g in range(grp)], axis=0).astype(BF16)
        s = lax.dot_general(qs, kh, (((1,), (1,)), ((), ())), preferred_element_type=F32) * ATTN_SCALE
        if valid is not None:
            s = jnp.where(valid, s, -jnp.inf)
        sink = jnp.concatenate([jnp.full((rows, 1), sink_ref[h * grp + g], F32) for g in range(grp)], axis=0)
        m = jnp.maximum(jnp.max(s, axis=-1, keepdims=True), sink)
        p = jnp.exp(s - m)
        den = jnp.sum(p, axis=-1, keepdims=True) + jnp.exp(sink - m)
        o = jnp.dot((p / den).astype(BF16), vh, preferred_element_type=F32)
        outs += [o[g * rows:(g + 1) * rows] for g in range(grp)]
    o_ref[...] = jnp.concatenate(outs, axis=1).astype(o_ref.dtype)


def _attn_prompt_body(sink_ref, q_ref, k0_ref, k1_ref, k2_ref, v0_ref, v1_ref, v2_ref, o_ref):
    i = pl.program_id(0)
    k_all = jnp.concatenate([k0_ref[...], k1_ref[...], k2_ref[...]], axis=0).astype(BF16)
    v_all = jnp.concatenate([v0_ref[...], v1_ref[...], v2_ref[...]], axis=0).astype(BF16)
    col = lax.broadcasted_iota(jnp.int32, (1, 3 * CHUNK), 1)
    valid = (col >= 2 * CHUNK) | ((col >= CHUNK) & (i >= 1)) | (i >= 2)
    _attend(q_ref[...], k_all, v_all, sink_ref, valid, o_ref)


def _attn_prompt(q_rot, k_rot, qkv, sinks, t):
    nc = t // CHUNK
    vcol = OFF_K // LANES
    kv = lambda d: (lambda i: (jnp.maximum(i - d, 0), 0))
    vv = lambda d: (lambda i: (jnp.maximum(i - d, 0), vcol))
    return pl.pallas_call(
        _attn_prompt_body, grid=(nc,),
        in_specs=[pl.BlockSpec(memory_space=pltpu.SMEM),
                  pl.BlockSpec((CHUNK, A_WIDTH), lambda i: (i, 0)),
                  pl.BlockSpec((CHUNK, LANES), kv(2)), pl.BlockSpec((CHUNK, LANES), kv(1)),
                  pl.BlockSpec((CHUNK, LANES), kv(0)),
                  pl.BlockSpec((CHUNK, LANES), vv(2)), pl.BlockSpec((CHUNK, LANES), vv(1)),
                  pl.BlockSpec((CHUNK, LANES), vv(0))],
        out_specs=pl.BlockSpec((CHUNK, A_WIDTH), lambda i: (i, 0)),
        out_shape=jax.ShapeDtypeStruct((t, A_WIDTH), BF16),
        compiler_params=_cp(("parallel",)), name="attn_prompt")(sinks, q_rot, k_rot, k_rot, k_rot, qkv, qkv, qkv)


def _attn_sample_body(sink_ref, q_ref, kn_ref, vn_ref, ck_ref, cv_ref, o_ref, nk_ref, nv_ref):
    s = q_ref.shape[0]
    kn, vn, ck, cv = kn_ref[...], vn_ref[...], ck_ref[0], cv_ref[0]
    k_all = jnp.concatenate([ck, kn], axis=0)
    v_all = jnp.concatenate([cv, vn], axis=0)
    _attend(q_ref[...], k_all.astype(BF16), v_all.astype(BF16), sink_ref, None, o_ref)
    nk_ref[0] = k_all[s:]
    nv_ref[0] = v_all[s:]


def _attn_sample(q_rot, k_rot, qkv, cache_k, cache_v, sinks, row0, nb, s):
    b0 = row0 // s
    vcol = OFF_K // LANES
    return pl.pallas_call(
        _attn_sample_body, grid=(nb,),
        in_specs=[pl.BlockSpec(memory_space=pltpu.SMEM),
                  pl.BlockSpec((s, A_WIDTH), lambda b: (b0 + b, 0)),
                  pl.BlockSpec((s, LANES), lambda b: (b0 + b, 0)),
                  pl.BlockSpec((s, LANES), lambda b: (b0 + b, vcol)),
                  pl.BlockSpec((1, WINDOW, LANES), lambda b: (b, 0, 0)),
                  pl.BlockSpec((1, WINDOW, LANES), lambda b: (b, 0, 0))],
        out_specs=[pl.BlockSpec((s, A_WIDTH), lambda b: (b, 0)),
                   pl.BlockSpec((1, WINDOW, LANES), lambda b: (b, 0, 0)),
                   pl.BlockSpec((1, WINDOW, LANES), lambda b: (b, 0, 0))],
        out_shape=[jax.ShapeDtypeStruct((nb * s, A_WIDTH), BF16),
                   jax.ShapeDtypeStruct((nb, WINDOW, LANES), F32),
                   jax.ShapeDtypeStruct((nb, WINDOW, LANES), F32)],
        compiler_params=_cp(("parallel",)), name="attn_sample")(sinks, q_rot, k_rot, qkv, cache_k, cache_v)


def _rw_prep_body(f_ref, fp_ref, mu_ref, w0_ref, a0_ref, kk_ref, ka_ref, w2_ref, a2_ref, g2_ref,
                  r_o, k_o, v_o, kk_o, a_o, lw_o, g_o, *, rolled):
    f = f_ref[...]
    if rolled:
        first = lax.broadcasted_iota(jnp.int32, f.shape, 0) == 0
        f_prev = jnp.where(first, fp_ref[0], pltpu.roll(f, 1, 0))
    else:
        f_prev = fp_ref[...]
    fs = f + (f_prev - f) * mu_ref[...]
    w = RW_WIDTH
    r, k, v = fs[:, 0:w], fs[:, w:2 * w], fs[:, 2 * w:3 * w]
    x3 = fs[:, 3 * w:]
    lw = jnp.dot(jnp.tanh(x3).astype(BF16), w2_ref[...], preferred_element_type=F32)
    la = jnp.dot(x3.astype(BF16), a2_ref[...], preferred_element_type=F32)
    g = jnp.dot(jax.nn.sigmoid(x3).astype(BF16), g2_ref[...], preferred_element_type=F32)
    z = -(w0_ref[...] + lw)
    w_log = -(jnp.maximum(z, 0.0) + jnp.log1p(jnp.exp(-jnp.abs(z)))) - 0.5
    a = jax.nn.sigmoid(a0_ref[...] + la)
    r_o[...] = r
    v_o[...] = v
    kk_o[...] = k * kk_ref[...]
    k_o[...] = k * (1.0 + (a - 1.0) * ka_ref[...])
    a_o[...] = a
    lw_o[...] = -jnp.exp(w_log)
    g_o[...] = g


def _rw_prep(f, prev, params, row0, m, tm, rolled):
    i0 = row0 // tm
    row = lambda width: pl.BlockSpec((tm, width), lambda i: (i, 0))
    par = lambda shape: pl.BlockSpec(shape, lambda i: (0, 0))
    prev_spec = pl.BlockSpec((1, 1, RW_PAD), lambda i: (i, 0, 0)) if rolled else row(RW_PAD)
    return pl.pallas_call(
        functools.partial(_rw_prep_body, rolled=rolled), grid=(m // tm,),
        in_specs=[pl.BlockSpec((tm, RW_PAD), lambda i: (i0 + i, 0)), prev_spec, par((1, RW_PAD))]
                 + [par((1, RW_WIDTH))] * 4 + [par((RW_LORA_PAD, RW_WIDTH))] * 3,
        out_specs=[row(RW_WIDTH)] * 7,
        out_shape=[jax.ShapeDtypeStruct((m, RW_WIDTH), F32)] * 7,
        compiler_params=_cp(("parallel",)), name="rwkv_prep")(f, prev, *params)


def _split(x):
    hi = x.astype(BF16)
    return hi, (x - hi.astype(F32)).astype(BF16)


_NN = ((1,), (0,))
_NT = ((1,), (1,))
_TN = ((0,), (0,))


def _dot3(a, b, dims):
    (ah, al), (bh, bl) = a, b
    f = lambda x, y: lax.dot_general(x, y, (dims, ((), ())), preferred_element_type=F32)
    return f(ah, bh) + (f(ah, bl) + f(al, bh))


def _rw_scan_body(r_ref, k_ref, v_ref, kk_ref, a_ref, lw_ref, g_ref, rk_ref, lnw_ref, lnb_ref, s0_ref,
                  o_ref, st_ref, s_scr, *, heads):
    c = pl.program_id(2)
    cs = r_ref.shape[0]
    n = RW_HEAD

    @pl.when(c == 0)
    def _():
        s_scr[...] = s0_ref[0]

    row = lax.broadcasted_iota(jnp.int32, (cs, cs), 0)
    col = lax.broadcasted_iota(jnp.int32, (cs, cs), 1)
    incl = row >= col
    strict = row > col
    ones_tril = incl.astype(BF16)
    steps = cs.bit_length() - 1
    hs = range(heads)
    sls = [slice(h * n, (h + 1) * n) for h in hs]
    r, k, v, kk, a, lw = ([ref[:, sl] for sl in sls] for ref in (r_ref, k_ref, v_ref, kk_ref, a_ref, lw_ref))
    s0 = [s_scr[h] for h in hs]
    kkn = [kk[h] / jnp.maximum(jnp.sqrt(jnp.sum(kk[h] * kk[h], axis=-1, keepdims=True)), 1e-12) for h in hs]
    lw_hi = [lw[h].astype(BF16) for h in hs]
    lw_mid = [(lw[h] - lw_hi[h].astype(F32)).astype(BF16) for h in hs]
    lw_lo = [(lw[h] - lw_hi[h].astype(F32) - lw_mid[h].astype(F32)).astype(BF16) for h in hs]
    tri = lambda part: jnp.dot(ones_tril, part, preferred_element_type=F32)
    cum = [tri(lw_lo[h]) + tri(lw_mid[h]) + tri(lw_hi[h]) for h in hs]
    cum_end = [cum[h][cs - 1:cs, :] for h in hs]
    e_neg = [jnp.exp(-cum[h]) for h in hs]
    e_end = [jnp.exp(cum_end[h] - cum[h]) for h in hs]
    a_t = [-kkn[h] * jnp.exp(cum[h] - lw[h]) for h in hs]
    r_t = [r[h] * jnp.exp(cum[h]) for h in hs]
    ar = [_split(jnp.concatenate([a_t[h], r_t[h]], axis=0)) for h in hs]
    b_s = [_split(kkn[h] * a[h] * e_neg[h]) for h in hs]
    k_s = [_split(k[h] * e_neg[h]) for h in hs]
    v_s = [_split(v[h]) for h in hs]
    s0_s = [_split(s0[h]) for h in hs]
    ab_rb = [_dot3(ar[h], b_s[h], _NT) for h in hs]
    ak_rk = [_dot3(ar[h], k_s[h], _NT) for h in hs]
    l_ak = [_split(jnp.where(strict, ak_rk[h][:cs], 0.0)) for h in hs]
    mv = [_dot3(l_ak[h], v_s[h], _NN) for h in hs]
    x = [jnp.concatenate([a_t[h], mv[h]], axis=1) for h in hs]
    lp = [jnp.where(strict, ab_rb[h][:cs], 0.0) for h in hs]
    for i in range(steps):
        lp_s = [_split(lp[h]) for h in hs]
        x = [x[h] + _dot3(lp_s[h], _split(x[h]), _NN) for h in hs]
        if i < steps - 1:
            lp = [_dot3(lp_s[h], lp_s[h], _NN) for h in hs]
    u = [_dot3(_split(x[h][:, :n]), s0_s[h], _NT) + x[h][:, n:] for h in hs]
    l_rb = [_split(jnp.where(incl, ab_rb[h][cs:], 0.0)) for h in hs]
    l_rk = [_split(jnp.where(incl, ak_rk[h][cs:], 0.0)) for h in hs]
    y = [_dot3((ar[h][0][cs:], ar[h][1][cs:]), s0_s[h], _NT) + _dot3(l_rk[h], v_s[h], _NN) for h in hs]
    y = [y[h] + _dot3(l_rb[h], _split(u[h]), _NN) for h in hs]
    uv = [_split(jnp.concatenate([u[h], v[h]], axis=0)) for h in hs]
    bk = [_split(jnp.concatenate([kkn[h] * a[h] * e_end[h], k[h] * e_end[h]], axis=0)) for h in hs]
    for h in hs:
        s_scr[h] = s0[h] * jnp.exp(cum_end[h]) + _dot3(uv[h], bk[h], _TN)
    outs = []
    for h in hs:
        mean = jnp.mean(y[h], axis=-1, keepdims=True)
        var = jnp.mean(jnp.square(y[h] - mean), axis=-1, keepdims=True)
        yn = (y[h] - mean) * lax.rsqrt(var + RW_LN_EPS) * lnw_ref[:, sls[h]] + lnb_ref[:, sls[h]]
        bonus = jnp.sum(r[h] * k[h] * rk_ref[:, sls[h]], axis=-1, keepdims=True) * v[h]
        outs.append((yn + bonus) * g_ref[:, sls[h]])
    o_ref[...] = jnp.concatenate(outs, axis=1).astype(o_ref.dtype)

    @pl.when(c == pl.num_programs(2) - 1)
    def _():
        st_ref[0] = s_scr[...]


def _rw_scan(prep, r_k, ln_w, ln_b, state0, row0, nb, t, cs, heads=24):
    nc = t // cs
    blk0 = row0 // cs
    width = heads * RW_HEAD
    seq = pl.BlockSpec((cs, width), lambda p, b, c: (blk0 + b * nc + c, p))
    par = pl.BlockSpec((1, width), lambda p, b, c: (0, p))
    st = pl.BlockSpec((1, heads, RW_HEAD, RW_HEAD), lambda p, b, c: (b, p, 0, 0))
    return pl.pallas_call(
        functools.partial(_rw_scan_body, heads=heads), grid=(RW_HEADS // heads, nb, nc),
        in_specs=[seq] * 7 + [par] * 3 + [st],
        out_specs=[pl.BlockSpec((cs, width), lambda p, b, c: (b * nc + c, p)), st],
        out_shape=[jax.ShapeDtypeStruct((nb * t, RW_WIDTH), BF16),
                   jax.ShapeDtypeStruct((nb, RW_HEADS, RW_HEAD, RW_HEAD), F32)],
        scratch_shapes=[pltpu.VMEM((heads, RW_HEAD, RW_HEAD), F32)],
        compiler_params=_cp(("parallel", "parallel", "arbitrary")), name=f"rwkv_scan_{cs}")(
            *prep, r_k, ln_w, ln_b, state0)


def _s5_body(u_ref, bm_ref, cm_ref, lp_ref, d_ref, x0_ref, y_ref, xt_ref, x_scr, cin_scr, carry_scr, *, exact):
    i = pl.program_id(2)
    tt = u_ref.shape[0]
    nb = tt // SUBLANES
    w = S5_TILE_ST

    @pl.when(i == 0)
    def _():
        carry_scr[...] = x0_ref[0, 0]

    u = u_ref[...]
    if exact:
        bu = jnp.dot(u, bm_ref[0], precision=HIGHEST, preferred_element_type=F32)
    else:
        bu = jnp.dot(u.astype(BF16), bm_ref[0].astype(BF16), preferred_element_type=F32)
    nct = w // LANES
    for ct in range(2 * nct):
        x_scr[ct] = bu[:, ct * LANES:(ct + 1) * LANES]

    def lam_pow(t, ct):
        return (lp_ref[0, t:t + 1, ct * LANES:(ct + 1) * LANES],
                lp_ref[0, t:t + 1, w + ct * LANES:w + (ct + 1) * LANES])

    def cmul_add(xr, xi, cr, ci, sr, si):
        return xr + cr * sr - ci * si, xi + cr * si + ci * sr

    def rows(ct, t):
        return x_scr.at[ct, pl.ds(t, nb, stride=SUBLANES), :]

    for ct in range(nct):
        lam_r, lam_i = lam_pow(0, ct)
        for t in range(1, SUBLANES):
            nr, ni = cmul_add(rows(ct, t)[...], rows(nct + ct, t)[...], lam_r, lam_i,
                              rows(ct, t - 1)[...], rows(nct + ct, t - 1)[...])
            rows(ct, t)[...] = nr
            rows(nct + ct, t)[...] = ni

    lam8 = jnp.concatenate([lp_ref[0, SUBLANES - 1:SUBLANES, :w], lp_ref[0, SUBLANES - 1:SUBLANES, w:]], axis=0)

    def block_carry(b, carry):
        last = b * SUBLANES + SUBLANES - 1
        for ct in range(2 * nct):
            half, lt = divmod(ct, nct)
            cin_scr[ct, pl.ds(b, 1), :] = carry[half:half + 1, lt * LANES:(lt + 1) * LANES]
        end_r = jnp.concatenate([x_scr[ct, pl.ds(last, 1), :] for ct in range(nct)], axis=1)
        end_i = jnp.concatenate([x_scr[nct + ct, pl.ds(last, 1), :] for ct in range(nct)], axis=1)
        nr, ni = cmul_add(end_r, end_i, lam8[0:1], lam8[1:2], carry[0:1], carry[1:2])
        return jnp.concatenate([nr, ni], axis=0)

    carry0 = jnp.concatenate([carry_scr[:, :w], carry_scr[:, w:]], axis=0)
    carry = lax.fori_loop(0, nb, block_carry, carry0)
    carry_scr[...] = jnp.concatenate([carry[0:1], carry[1:2]], axis=1)
    for ct in range(nct):
        cin_r, cin_i = cin_scr[ct], cin_scr[nct + ct]
        for t in range(SUBLANES):
            pr, pi_ = lam_pow(t, ct)
            nr, ni = cmul_add(rows(ct, t)[...], rows(nct + ct, t)[...], pr, pi_, cin_r, cin_i)
            rows(ct, t)[...] = nr
            rows(nct + ct, t)[...] = ni

    xs = jnp.concatenate([x_scr[ct] for ct in range(2 * nct)], axis=1)
    if exact:
        y = jnp.dot(xs, cm_ref[0], precision=HIGHEST, preferred_element_type=F32)
    else:
        y = jnp.dot(xs.astype(BF16), cm_ref[0].astype(BF16), preferred_element_type=F32)
    y_ref[...] = y + d_ref[...] * u

    @pl.when(i == pl.num_programs(2) - 1)
    def _():
        xt_ref[0, 0] = carry_scr[...]


def _s5_scan(u, bm, cm, lp, d, x0, row0, nb_seq, t, tt, exact):
    nt = t // tt
    i0 = row0 // tt
    w2 = 2 * S5_TILE_ST
    return pl.pallas_call(
        functools.partial(_s5_body, exact=exact), grid=(S5_TILES, nb_seq, nt),
        in_specs=[pl.BlockSpec((tt, S5_TILE_IN), lambda j, b, i: (i0 + b * nt + i, j)),
                  pl.BlockSpec((1, S5_TILE_IN, w2), lambda j, b, i: (j, 0, 0)),
                  pl.BlockSpec((1, w2, S5_TILE_IN), lambda j, b, i: (j, 0, 0)),
                  pl.BlockSpec((1, SUBLANES, w2), lambda j, b, i: (j, 0, 0)),
                  pl.BlockSpec((1, S5_TILE_IN), lambda j, b, i: (0, j)),
                  pl.BlockSpec((1, 1, 1, w2), lambda j, b, i: (b, j, 0, 0))],
        out_specs=[pl.BlockSpec((tt, S5_TILE_IN), lambda j, b, i: (b * nt + i, j)),
                   pl.BlockSpec((1, 1, 1, w2), lambda j, b, i: (b, j, 0, 0))],
        out_shape=[jax.ShapeDtypeStruct((nb_seq * t, S5_WIDTH), F32),
                   jax.ShapeDtypeStruct((nb_seq, S5_TILES, 1, w2), F32)],
        scratch_shapes=[pltpu.VMEM((w2 // LANES, tt, LANES), F32), pltpu.VMEM((w2 // LANES, tt // SUBLANES, LANES), F32),
                        pltpu.VMEM((1, w2), F32)],
        compiler_params=_cp(("parallel", "parallel", "arbitrary")), name=f"s5_scan_{tt}")(u, bm, cm, lp, d, x0)


def _gelu_tanh(y):
    return 0.5 * y * (1.0 + jnp.tanh(math.sqrt(2.0 / math.pi) * (y + 0.044715 * (y * y * y))))


def _s5_glu_body(y_ref, w_ref, b_ref, o_ref, *, tn):
    j = pl.program_id(1)
    z = _gelu_tanh(y_ref[...])
    gate = jax.nn.sigmoid(jnp.dot(z.astype(BF16), w_ref[...], preferred_element_type=F32) + b_ref[...])
    zj = _gelu_tanh(y_ref[:, pl.ds(pl.multiple_of(j * tn, tn), tn)])
    o_ref[...] = (zj * gate).astype(o_ref.dtype)


def _s5_glu(y, w, b, tn=512):
    m, d = y.shape
    tm = _pick(m, (256, 128, 16, 8))
    return pl.pallas_call(
        functools.partial(_s5_glu_body, tn=tn), grid=(m // tm, d // tn),
        in_specs=[pl.BlockSpec((tm, d), lambda i, j: (i, 0)), pl.BlockSpec((d, tn), lambda i, j: (0, j)),
                  pl.BlockSpec((1, tn), lambda i, j: (0, j))],
        out_specs=pl.BlockSpec((tm, tn), lambda i, j: (i, j)),
        out_shape=jax.ShapeDtypeStruct((m, d), BF16),
        compiler_params=_cp(("parallel", "parallel")), name="s5_glu")(y, w, b)


def _merge_body(xn_ref, oa_ref, ob_ref, oc_ref, ga_ref, gb_ref, gc_ref, wa_ref, wb_ref, wc_ref, o_ref):
    xn = xn_ref[...]
    gate = lambda g_ref: jax.nn.sigmoid(jnp.dot(xn, g_ref[...], preferred_element_type=F32))
    acc = gate(ga_ref) * jnp.dot(oa_ref[...], wa_ref[...], preferred_element_type=F32)
    acc = acc + gate(gb_ref) * jnp.dot(ob_ref[...], wb_ref[...], preferred_element_type=F32)
    acc = acc + gate(gc_ref) * jnp.dot(oc_ref[...], wc_ref[...], preferred_element_type=F32)
    o_ref[...] = acc.astype(o_ref.dtype)


def _merge(xn, w_pk, gate_col0, o_a, o_b, o_c, w_a, w_b, w_c, tn=256):
    m, d = xn.shape
    tm = _row_tile(m)
    nj = D_MODEL // tn
    j0 = gate_col0 // tn
    full = lambda width: pl.BlockSpec((tm, width), lambda i, j: (i, 0))
    gate = lambda which: pl.BlockSpec((d, tn), lambda i, j: (0, j0 + which * nj + j))
    wcol = lambda rows: pl.BlockSpec((rows, tn), lambda i, j: (0, j))
    return pl.pallas_call(
        _merge_body, grid=(m // tm, nj),
        in_specs=[full(d), full(A_WIDTH), full(RW_WIDTH), full(S5_WIDTH), gate(0), gate(1), gate(2),
                  wcol(A_WIDTH), wcol(RW_WIDTH), wcol(S5_WIDTH)],
        out_specs=pl.BlockSpec((tm, tn), lambda i, j: (i, j)),
        out_shape=jax.ShapeDtypeStruct((m, D_MODEL), BF16),
        compiler_params=_cp(("parallel", "parallel")), name="merge")(
            xn, o_a, o_b, o_c, w_pk, w_pk, w_pk, w_a, w_b, w_c)


def _expert_body(te_ref, nu_ref, tok_ref, x_hbm, wg_ref, wu_ref, wd_ref, o_ref, xrows, xb, hg_acc, hu_acc, hb, sem):
    i, j = pl.program_id(0), pl.program_id(1)
    nu = nu_ref[0]
    used = i < nu

    def row_copy(tile, r):
        return pltpu.make_async_copy(x_hbm.at[pl.ds(tok_ref[tile * MOE_TM + r], 1)], xrows.at[pl.ds(r, 1)], sem.at[0])

    def start_gather(tile):
        def body(r, c):
            row_copy(tile, r).start()
            return c
        lax.fori_loop(0, MOE_TM, body, 0)

    def wait_gather(tile):
        def body(r, c):
            row_copy(tile, r).wait()
            return c
        lax.fori_loop(0, MOE_TM, body, 0)

    @pl.when(jnp.logical_and(j == 0, jnp.logical_and(i == 0, nu > 0)))
    def _():
        start_gather(0)

    @pl.when(jnp.logical_and(j == 0, used))
    def _():
        wait_gather(i)
        xb[...] = xrows[...].astype(BF16)

    @pl.when(jnp.logical_and(j == 0, i + 1 < nu))
    def _():
        start_gather(i + 1)

    down = j >= MOE_KSTEPS

    @pl.when(jnp.logical_and(used, jnp.logical_not(down)))
    def _():
        x = xb[:, pl.ds(pl.multiple_of(j * MOE_TK, MOE_TK), MOE_TK)]
        pg = jnp.dot(x, wg_ref[...].astype(BF16), preferred_element_type=F32)
        pu = jnp.dot(x, wu_ref[...].astype(BF16), preferred_element_type=F32)

        @pl.when(j == 0)
        def _():
            hg_acc[...] = pg
            hu_acc[...] = pu

        @pl.when(j > 0)
        def _():
            hg_acc[...] += pg
            hu_acc[...] += pu

    @pl.when(jnp.logical_and(used, j == MOE_KSTEPS))
    def _():
        hg = hg_acc[...]
        hb[...] = (hg * jax.nn.sigmoid(hg) * hu_acc[...]).astype(BF16)

    @pl.when(jnp.logical_and(used, down))
    def _():
        o_ref[...] = jnp.dot(hb[...], wd_ref[...].astype(BF16), preferred_element_type=F32)

    @pl.when(jnp.logical_and(jnp.logical_not(used), down))
    def _():
        o_ref[...] = jnp.zeros_like(o_ref)


def _experts(x, slot_tok, tile_expert, n_used, layer, w_gate, w_up, w_down):
    d = x.shape[1]
    n_tiles = slot_tok.shape[0] // MOE_TM
    n_down = d // MOE_TN

    def wmap(i, j, te, nu, tok):
        live = i < nu[0]
        return layer, te[jnp.minimum(i, nu[0] - 1)], jnp.where(live, jnp.minimum(j, MOE_KSTEPS - 1), MOE_KSTEPS - 1), 0

    def wdmap(i, j, te, nu, tok):
        live = i < nu[0]
        return layer, te[jnp.minimum(i, nu[0] - 1)], 0, jnp.where(live, jnp.maximum(j - MOE_KSTEPS, 0), n_down - 1)

    grid_spec = pltpu.PrefetchScalarGridSpec(
        num_scalar_prefetch=3, grid=(n_tiles, MOE_KSTEPS + n_down),
        in_specs=[pl.BlockSpec(memory_space=pl.ANY),
                  pl.BlockSpec((None, None, MOE_TK, D_EXPERT), wmap), pl.BlockSpec((None, None, MOE_TK, D_EXPERT), wmap),
                  pl.BlockSpec((None, None, D_EXPERT, MOE_TN), wdmap)],
        out_specs=pl.BlockSpec((MOE_TM, MOE_TN), lambda i, j, te, nu, tok: (i, jnp.maximum(j - MOE_KSTEPS, 0))),
        scratch_shapes=[pltpu.VMEM((MOE_TM, d), F32), pltpu.VMEM((MOE_TM, d), BF16),
                        pltpu.VMEM((MOE_TM, D_EXPERT), F32), pltpu.VMEM((MOE_TM, D_EXPERT), F32),
                        pltpu.VMEM((MOE_TM, D_EXPERT), BF16), pltpu.SemaphoreType.DMA((1,))])
    return pl.pallas_call(
        _expert_body, grid_spec=grid_spec, out_shape=jax.ShapeDtypeStruct((n_tiles * MOE_TM, d), F32),
        compiler_params=_cp(("arbitrary", "arbitrary"), 56), name="experts")(
            tile_expert, n_used, slot_tok, x, w_gate, w_up, w_down)


def _combine_body(slot_ref, x_ref, g_ref, out_hbm, o_ref, buf, sem, *, tc):
    i = pl.program_id(0)
    b = i % 2

    def row_copy(tile, bb, r):
        return pltpu.make_async_copy(out_hbm.at[pl.ds(slot_ref[tile * 2 * tc + r], 1)], buf.at[bb, pl.ds(r, 1)],
                                     sem.at[bb])

    def start_gather(tile, bb):
        def body(r, c):
            row_copy(tile, bb, r).start()
            return c
        lax.fori_loop(0, 2 * tc, body, 0)

    @pl.when(i == 0)
    def _():
        start_gather(0, 0)

    @pl.when(i + 1 < pl.num_programs(0))
    def _():
        start_gather(i + 1, 1 - b)

    def wait_body(r, c):
        row_copy(i, b, r).wait()
        return c
    lax.fori_loop(0, 2 * tc, wait_body, 0)
    rows = buf[b]
    g = g_ref[...]
    o_ref[...] = x_ref[...] + (g[:, 0:1] * rows[:tc] + g[:, 1:2] * rows[tc:])


def _combine(x, gate, slot, out):
    n, d = x.shape
    tc = _pick(n, (128, 96, 64, 32, 16, 8))
    tiles = n // tc
    slot_flat = slot.reshape(tiles, tc, TOP_K).transpose(0, 2, 1).reshape(-1)
    grid_spec = pltpu.PrefetchScalarGridSpec(
        num_scalar_prefetch=1, grid=(tiles,),
        in_specs=[pl.BlockSpec((tc, d), lambda i, s: (i, 0)), pl.BlockSpec((tc, TOP_K), lambda i, s: (i, 0)),
                  pl.BlockSpec(memory_space=pl.ANY)],
        out_specs=pl.BlockSpec((tc, d), lambda i, s: (i, 0)),
        scratch_shapes=[pltpu.VMEM((2, TOP_K * tc, d), F32), pltpu.SemaphoreType.DMA((2,))])
    return pl.pallas_call(
        functools.partial(_combine_body, tc=tc), grid_spec=grid_spec, out_shape=jax.ShapeDtypeStruct((n, d), F32),
        compiler_params=_cp(("arbitrary",)), name="moe_combine")(slot_flat, x, gate, out)


def _moe(x, xn, rw, rg_b, re_b, layer, w_gate, w_up, w_down):
    n, d = xn.shape
    logits = _matmul(xn, rw, tn=LANES, name="router")
    g_logits = logits[:, :N_GROUPS] + rg_b
    grp = jnp.argmax(g_logits, axis=-1).astype(jnp.int32)
    p_grp = jnp.take_along_axis(jax.nn.softmax(g_logits, axis=-1), grp[:, None], axis=-1)
    e_logits = (logits[:, N_GROUPS:N_GROUPS + N_EXPERTS] + re_b).reshape(n, N_GROUPS, EXPERTS_PER_GROUP)
    e_logits = jnp.take_along_axis(e_logits, grp[:, None, None], axis=1)[:, 0]
    top_v, top_i = lax.top_k(e_logits, TOP_K)
    gate = jax.nn.softmax(top_v, axis=-1) * p_grp
    expert_id = (grp[:, None] * EXPERTS_PER_GROUP + top_i).astype(jnp.int32).reshape(-1)

    a = n * TOP_K
    n_tiles = -(-(a + N_EXPERTS * (MOE_TM - 1)) // MOE_TM)
    idx = jnp.arange(a, dtype=jnp.int32)
    order = jnp.argsort(expert_id * a + idx)
    e_sorted = expert_id[order]
    counts = jnp.bincount(expert_id, length=N_EXPERTS).astype(jnp.int32)
    starts = jnp.cumsum(counts) - counts
    padded = (counts + MOE_TM - 1) // MOE_TM * MOE_TM
    pad_end = jnp.cumsum(padded)
    pad_start = pad_end - padded
    slot_sorted = (pad_start[e_sorted] + idx - starts[e_sorted]).astype(jnp.int32)
    slot = jnp.zeros((a,), jnp.int32).at[order].set(slot_sorted)
    slot_tok = jnp.zeros((n_tiles * MOE_TM,), jnp.int32).at[slot].set(idx // TOP_K)
    tile_expert = jnp.minimum(
        jnp.searchsorted(pad_end, jnp.arange(n_tiles, dtype=jnp.int32) * MOE_TM, side='right'),
        N_EXPERTS - 1).astype(jnp.int32)
    n_used = (pad_end[-1:] // MOE_TM).astype(jnp.int32)
    out = _experts(xn, slot_tok, tile_expert, n_used, layer, w_gate, w_up, w_down)
    return _combine(x, gate, slot.reshape(n, TOP_K), out)


def _s5_params(lam_re, lam_im, b_re, b_im, c_re, c_im, log_step):
    dt = jnp.exp(log_step)[:, None]
    mag = jnp.exp(lam_re * dt)
    lbr, lbi = mag * jnp.cos(lam_im * dt), mag * jnp.sin(lam_im * dt)
    den = lam_re * lam_re + lam_im * lam_im
    qr = ((lbr - 1.0) * lam_re + lbi * lam_im) / den
    qi = (lbi * lam_re - (lbr - 1.0) * lam_im) / den
    bbr = qr[:, :, None] * b_re - qi[:, :, None] * b_im
    bbi = qr[:, :, None] * b_im + qi[:, :, None] * b_re
    eye = jnp.eye(S5_TILE_GROUPS, dtype=F32)

    def blockdiag_in(m):
        m = m.reshape(S5_TILES, S5_TILE_GROUPS, S5_STATE, S5_GROUP)
        return jnp.einsum('jgph,ge->jghep', m, eye).reshape(S5_TILES, S5_TILE_IN, S5_TILE_ST)

    def blockdiag_out(m):
        m = m.reshape(S5_TILES, S5_TILE_GROUPS, S5_GROUP, S5_STATE)
        return jnp.einsum('jghp,ge->jgpeh', m, eye).reshape(S5_TILES, S5_TILE_ST, S5_TILE_IN)

    bm = jnp.concatenate([blockdiag_in(bbr), blockdiag_in(bbi)], axis=2)
    cm = jnp.concatenate([blockdiag_out(c_re), blockdiag_out(-c_im)], axis=1)
    pr, pi_ = [lbr], [lbi]
    for _ in range(SUBLANES - 1):
        pr, pi_ = pr + [pr[-1] * lbr - pi_[-1] * lbi], pi_ + [pr[-1] * lbi + pi_[-1] * lbr]
    tile = lambda p: jnp.stack(p, 0).reshape(SUBLANES, S5_TILES, S5_TILE_ST).transpose(1, 0, 2)
    lp = jnp.concatenate([tile(pr), tile(pi_)], axis=2)
    return bm, cm, lp


def _s5_state_in(re, im):
    b = re.shape[0]
    f = lambda x: x.reshape(b, S5_TILES, 1, S5_TILE_ST)
    return jnp.concatenate([f(re), f(im)], axis=3)


def _s5_state_out(x):
    b = x.shape[0]
    return (x[..., 0, :S5_TILE_ST].reshape(b, S5_GROUPS, S5_STATE), x[..., 0, S5_TILE_ST:].reshape(b, S5_GROUPS, S5_STATE))


def _pad_rows(w, rows, at):
    return jnp.zeros((rows, w.shape[1]), w.dtype).at[at:at + w.shape[0]].set(w)


def _layer(x, tp, nbs, ts, rope_cos, rope_sin, cache_k, cache_v, shift0, rw0, s5re0, s5im0, lw, stacked):
    n = x.shape[0]
    bf = lambda w: w.astype(BF16)
    layer, w_in_all = stacked[0], stacked[1]
    experts = (layer,) + tuple(stacked[2:])
    w_pk = _repack_w_in(w_in_all, layer)
    gate_col0 = OFF_V + RW_PAD + S5_WIDTH
    pk_s5 = OFF_V + RW_PAD
    xn = _rmsnorm(x, lw['norm_mix_g'], BF16)
    qkv = _matmul(xn, w_pk, col0=0, n=OFF_V, tn=256, name="proj_qkv")
    f = _matmul(xn, w_pk, col0=OFF_V, n=RW_PAD, tn=256, name="proj_rwkv")
    u = _matmul(xn, w_pk, col0=pk_s5, n=S5_WIDTH, name="proj_s5")

    q_rot, k_rot = _rope(qkv, rope_cos, rope_sin)
    o_ap = _attn_prompt(q_rot, k_rot, qkv, lw['attn_sinks'], tp)
    ck = cache_k.reshape(nbs, WINDOW, A_KV_WIDTH)
    cv = cache_v.reshape(nbs, WINDOW, A_KV_WIDTH)
    o_as, nk_s, nv_s = _attn_sample(q_rot, k_rot, qkv, ck, cv, lw['attn_sinks'], tp, nbs, ts)
    o_a = jnp.concatenate([o_ap, o_as], axis=0)
    kv_shape = (-1, WINDOW, A_KV_HEADS, HEAD_DIM)
    nk_p = k_rot[tp - WINDOW:tp].reshape(kv_shape)
    nv_p = qkv[tp - WINDOW:tp, OFF_K:OFF_V].reshape(kv_shape)

    ms = nbs * ts
    tm_p = _pick(tp, (256, 128, 64))
    assert tp % ms == 0, "sample rows must start on a tile boundary"
    above = jnp.concatenate([jnp.zeros((1, RW_PAD), F32), f[tm_p - 1:tp - 1:tm_p]], axis=0)[:, None, :]
    f_s = f[tp:].reshape(nbs, ts, RW_PAD)
    sh = jnp.pad(shift0, ((0, 0), (0, RW_PAD - SHIFT_WIDTH)))
    fp_s = jnp.concatenate([sh[:, None, :], f_s[:, :-1]], axis=1).reshape(ms, RW_PAD)
    row = lambda v: v.reshape(1, -1)
    prep_par = (jnp.pad(row(lw['rwkv_mu']), ((0, 0), (0, RW_PAD - SHIFT_WIDTH))), row(lw['rwkv_w0']),
                row(lw['rwkv_a0']), row(lw['rwkv_k_k']), row(lw['rwkv_k_a']),
                bf(_pad_rows(lw['rwkv_w2'], RW_LORA_PAD, 0)), bf(_pad_rows(lw['rwkv_a2'], RW_LORA_PAD, RW_LORA)),
                bf(_pad_rows(lw['rwkv_g2'], RW_LORA_PAD, 2 * RW_LORA)))
    prep_p = _rw_prep(f, above, prep_par, 0, tp, tm_p, True)
    prep_s = _rw_prep(f, fp_s, prep_par, tp, ms, ms, False)
    rw_par = (row(lw['rwkv_r_k']), row(lw['rwkv_ln_w']), row(lw['rwkv_ln_b']))
    zero_rw = jnp.zeros((1, RW_HEADS, RW_HEAD, RW_HEAD), F32)
    o_bp, rw_p = _rw_scan(prep_p, *rw_par, zero_rw, 0, 1, tp, CHUNK)
    o_bs, rw_s = _rw_scan(prep_s, *rw_par, rw0, 0, nbs, ts, ts)
    o_b = jnp.concatenate([o_bp, o_bs], axis=0)
    shift_p = f[tp - 1:tp, :SHIFT_WIDTH]
    shift_s = f_s[:, -1, :SHIFT_WIDTH]

    bm, cm, lp = _s5_params(lw['s5_lambda_re'], lw['s5_lambda_im'], lw['s5_b_re'], lw['s5_b_im'],
                            lw['s5_c_re'], lw['s5_c_im'], lw['s5_log_step'])
    d_row = row(lw['s5_d'])
    zero_s5 = jnp.zeros((1, S5_TILES, 1, 2 * S5_TILE_ST), F32)
    y_p, xs_p = _s5_scan(u, bm, cm, lp, d_row, zero_s5, 0, 1, tp, tm_p, False)
    y_s, xs_s = _s5_scan(u, bm, cm, lp, d_row, _s5_state_in(s5re0, s5im0), tp, nbs, ts, ts, True)
    o_c = _s5_glu(jnp.concatenate([y_p, y_s], axis=0), bf(lw['s5_glu_w']), row(lw['s5_glu_b']))
    s5re_p, s5im_p = _s5_state_out(xs_p)
    s5re_s, s5im_s = _s5_state_out(xs_s)

    merged = _merge(xn, w_pk, gate_col0, o_a, o_b, o_c,
                    bf(lw['w_branch_attn']), bf(lw['w_branch_rwkv']), bf(lw['w_branch_s5']))
    x = _matmul_residual(merged, bf(lw['w_out']), x)

    xn2 = _rmsnorm(x, lw['norm_ffn_g'], F32)
    rw = jnp.concatenate([lw['router_group_w'], lw['router_expert_w'],
                          jnp.zeros((D_MODEL, LANES - N_GROUPS - N_EXPERTS), F32)], axis=1)
    x = _moe(x, xn2, bf(rw), lw['router_group_b'], lw['router_expert_b'], *experts)
    st_p = (nk_p, nv_p, shift_p, rw_p, s5re_p, s5im_p)
    st_s = (nk_s.reshape(kv_shape), nv_s.reshape(kv_shape), shift_s, rw_s, s5re_s, s5im_s)
    return x, st_p, st_s


def _rope_tables(pos):
    half = HEAD_DIM // 2
    inv_freq = ROPE_THETA ** (-jnp.arange(half, dtype=F32) / half)
    ang = pos.astype(F32)[:, None] * inv_freq[None, :]
    cos, sin = jnp.cos(ang), jnp.sin(ang)
    reps = LANES // HEAD_DIM
    return jnp.tile(jnp.concatenate([cos, cos], axis=1), (1, reps)), jnp.tile(jnp.concatenate([-sin, sin], axis=1), (1, reps))


_LAYER_KEYS = ('norm_mix_g', 'w_in', 'attn_sinks', 'rwkv_mu', 'rwkv_w0', 'rwkv_w2', 'rwkv_a0', 'rwkv_a2', 'rwkv_g2',
               'rwkv_k_k', 'rwkv_k_a', 'rwkv_r_k', 'rwkv_ln_w', 'rwkv_ln_b', 's5_lambda_re', 's5_lambda_im',
               's5_b_re', 's5_b_im', 's5_c_re', 's5_c_im', 's5_d', 's5_log_step', 's5_glu_w', 's5_glu_b',
               'w_branch_attn', 'w_branch_rwkv', 'w_branch_s5', 'w_out', 'norm_ffn_g', 'router_group_w',
               'router_group_b', 'router_expert_w', 'router_expert_b', 'expert_w_gate', 'expert_w_up', 'expert_w_down')


def _forward(x_prompt, x_sample, cache_k, cache_v, state_shift, state_rwkv, state_s5_re, state_s5_im,
             layer_weights, norm_final_g):
    bp, tp, d = x_prompt.shape
    assert bp == 1, "the prompt batch is one new stream"
    nbs, ts, _ = x_sample.shape
    depth = cache_k.shape[0]
    x = jnp.concatenate([x_prompt.reshape(tp, d), x_sample.reshape(nbs * ts, d)], axis=0)
    pos = jnp.concatenate([jnp.arange(tp, dtype=jnp.int32),
                           jnp.tile(PAST_LEN + jnp.arange(ts, dtype=jnp.int32), nbs)])
    rope_cos, rope_sin = _rope_tables(pos)
    new_p = [[] for _ in range(6)]
    new_s = [[] for _ in range(6)]
    w_in_bf = layer_weights[_LAYER_KEYS.index('w_in')].astype(BF16)
    for l in range(depth):
        big = ('w_in', 'expert_w_gate', 'expert_w_up', 'expert_w_down')
        lw = {k: v[l] for k, v in zip(_LAYER_KEYS, layer_weights) if k not in big}
        stacked = (l, w_in_bf) + tuple(layer_weights[_LAYER_KEYS.index(k)] for k in big[1:])
        x, st_p, st_s = _layer(x, tp, nbs, ts, rope_cos, rope_sin, cache_k[l], cache_v[l], state_shift[l],
                               state_rwkv[l], state_s5_re[l], state_s5_im[l], lw, stacked)
        for i in range(6):
            new_p[i].append(st_p[i])
            new_s[i].append(st_s[i])
    y = _rmsnorm(x, norm_final_g, F32)
    outs_p = tuple(jnp.stack(t, axis=0) for t in new_p)
    outs_s = tuple(jnp.stack(t, axis=0) for t in new_s)
    return (y[:tp].reshape(1, tp, d), y[tp:].reshape(nbs, ts, d)) + outs_p + outs_s


def kernel(x_prompt, x_sample, cache_k, cache_v, state_shift, state_rwkv, state_s5_re, state_s5_im,
           norm_mix_g, w_in, attn_sinks, rwkv_mu, rwkv_w0, rwkv_w2, rwkv_a0, rwkv_a2, rwkv_g2,
           rwkv_k_k, rwkv_k_a, rwkv_r_k, rwkv_ln_w, rwkv_ln_b, s5_lambda_re, s5_lambda_im,
           s5_b_re, s5_b_im, s5_c_re, s5_c_im, s5_d, s5_log_step, s5_glu_w, s5_glu_b,
           w_branch_attn, w_branch_rwkv, w_branch_s5, w_out, norm_ffn_g, router_group_w,
           router_group_b, router_expert_w, router_expert_b, expert_w_gate, expert_w_up,
           expert_w_down, norm_final_g):
    layer_weights = (norm_mix_g, w_in, attn_sinks, rwkv_mu, rwkv_w0, rwkv_w2, rwkv_a0, rwkv_a2, rwkv_g2,
                     rwkv_k_k, rwkv_k_a, rwkv_r_k, rwkv_ln_w, rwkv_ln_b, s5_lambda_re, s5_lambda_im,
                     s5_b_re, s5_b_im, s5_c_re, s5_c_im, s5_d, s5_log_step, s5_glu_w, s5_glu_b,
                     w_branch_attn, w_branch_rwkv, w_branch_s5, w_out, norm_ffn_g, router_group_w,
                     router_group_b, router_expert_w, router_expert_b, expert_w_gate, expert_w_up, expert_w_down)
    return _forward(x_prompt, x_sample, cache_k, cache_v, state_shift, state_rwkv, state_s5_re, state_s5_im,
                    layer_weights, norm_final_g)
```

```python
import functools
import math

import jax
import jax.numpy as jnp
from jax import lax
from jax.experimental import pallas as pl
from jax.experimental.pallas import tpu as pltpu

F32 = jnp.float32
BF16 = jnp.bfloat16
HIGHEST = lax.Precision.HIGHEST

D_MODEL = 4096
CHUNK = 64
NORM_EPS = 1e-5
A_HEADS = 16
A_KV_HEADS = 2
HEAD_DIM = 64
A_WIDTH = A_HEADS * HEAD_DIM
A_KV_WIDTH = A_KV_HEADS * HEAD_DIM
WINDOW = 128
ROPE_THETA = 10000.0
ATTN_SCALE = HEAD_DIM ** -0.5
PAST_LEN = 4096
RW_HEAD = 64
RW_HEADS = 24
RW_WIDTH = RW_HEADS * RW_HEAD
RW_LORA = 64
RW_LN_EPS = 64e-5
SHIFT_WIDTH = 3 * RW_WIDTH + 3 * RW_LORA
S5_GROUPS = 96
S5_GROUP = 16
S5_WIDTH = S5_GROUPS * S5_GROUP
S5_STATE = 64
OFF_Q = A_WIDTH
OFF_K = OFF_Q + A_KV_WIDTH
OFF_V = OFF_K + A_KV_WIDTH
OFF_RW = OFF_V + SHIFT_WIDTH
OFF_S5 = OFF_RW + S5_WIDTH
OFF_GA = OFF_S5 + D_MODEL
OFF_GB = OFF_GA + D_MODEL
IN_COLS = OFF_GB + D_MODEL
N_GROUPS = 4
EXPERTS_PER_GROUP = 8
N_EXPERTS = N_GROUPS * EXPERTS_PER_GROUP
TOP_K = 2
D_EXPERT = 1024

LANES = 128
SUBLANES = 8
RW_PAD = 4864
RW_LORA_PAD = RW_PAD - 3 * RW_WIDTH
S5_TILE_GROUPS = 8
S5_TILE_IN = S5_TILE_GROUPS * S5_GROUP
S5_TILE_ST = S5_TILE_GROUPS * S5_STATE
S5_TILES = S5_GROUPS // S5_TILE_GROUPS
MOE_TM = 640
MOE_TK = 512
MOE_KSTEPS = D_MODEL // MOE_TK
MOE_TN = 512


def _cp(sem, vmem_mb=48):
    return pltpu.CompilerParams(dimension_semantics=sem, vmem_limit_bytes=vmem_mb * 2 ** 20)


def _pick(n, prefs):
    for p in prefs:
        if n % p == 0:
            return p
    raise ValueError(f"no tile for {n} in {prefs}")


def _rmsnorm_body(x_ref, g_ref, o_ref):
    x = x_ref[...]
    inv = lax.rsqrt(jnp.mean(x * x, axis=-1, keepdims=True) + NORM_EPS)
    o_ref[...] = (x * inv * g_ref[...]).astype(o_ref.dtype)


def _rmsnorm(x, g, out_dtype):
    m, d = x.shape
    tm = _pick(m, (256, 128, 16, 8))
    return pl.pallas_call(
        _rmsnorm_body, grid=(m // tm,),
        in_specs=[pl.BlockSpec((tm, d), lambda i: (i, 0)), pl.BlockSpec((1, d), lambda i: (0, 0))],
        out_specs=pl.BlockSpec((tm, d), lambda i: (i, 0)),
        out_shape=jax.ShapeDtypeStruct((m, d), out_dtype),
        compiler_params=_cp(("parallel",)), name="rmsnorm")(x, g.reshape(1, d))


def _mm_body(a_ref, b_ref, o_ref, *, act):
    acc = jnp.dot(a_ref[...].astype(BF16), b_ref[...], preferred_element_type=F32)
    if act == "sigmoid":
        acc = jax.nn.sigmoid(acc)
    o_ref[...] = acc.astype(o_ref.dtype)


def _mm_res_body(a_ref, b_ref, r_ref, o_ref):
    o_ref[...] = r_ref[...] + jnp.dot(a_ref[...], b_ref[...], preferred_element_type=F32)


def _row_tile(m):
    return _pick(m, (768, 512, 256, 128, 64, 16, 8))


def _matmul(a, b, *, col0=0, n=None, act=None, out_dtype=F32, tn=512, name="matmul"):
    m, k = a.shape
    n = b.shape[1] if n is None else n
    tm = _row_tile(m)
    tn = _pick(math.gcd(n, col0) if col0 else n, (tn, 256, 128))
    j0 = col0 // tn
    return pl.pallas_call(
        functools.partial(_mm_body, act=act), grid=(m // tm, n // tn),
        in_specs=[pl.BlockSpec((tm, k), lambda i, j: (i, 0)), pl.BlockSpec((k, tn), lambda i, j: (0, j0 + j))],
        out_specs=pl.BlockSpec((tm, tn), lambda i, j: (i, j)),
        out_shape=jax.ShapeDtypeStruct((m, n), out_dtype),
        compiler_params=_cp(("parallel", "parallel")), name=name)(a, b)


def _matmul_residual(a, b, res, *, tn=512):
    m, k = a.shape
    n = b.shape[1]
    tm = _row_tile(m)
    return pl.pallas_call(
        _mm_res_body, grid=(m // tm, n // tn),
        in_specs=[pl.BlockSpec((tm, k), lambda i, j: (i, 0)), pl.BlockSpec((k, tn), lambda i, j: (0, j)),
                  pl.BlockSpec((tm, tn), lambda i, j: (i, j))],
        out_specs=pl.BlockSpec((tm, tn), lambda i, j: (i, j)),
        out_shape=jax.ShapeDtypeStruct((m, n), F32),
        compiler_params=_cp(("parallel", "parallel")), name="matmul_residual")(a, b, res)


def _repack_body(w_ref, o_ref):
    w = w_ref[...].astype(F32)
    pad = jnp.zeros((w.shape[0], RW_PAD - SHIFT_WIDTH), F32)
    o_ref[...] = jnp.concatenate([w[:, :OFF_RW], pad, w[:, OFF_RW:]], axis=1).astype(o_ref.dtype)


def _repack_w_in(w_in_all, layer):
    _, k, n = w_in_all.shape
    tr = 64
    n_out = n + RW_PAD - SHIFT_WIDTH
    return pl.pallas_call(
        _repack_body, grid=(k // tr,),
        in_specs=[pl.BlockSpec((None, tr, n), lambda i: (layer, i, 0))],
        out_specs=pl.BlockSpec((tr, n_out), lambda i: (i, 0)),
        out_shape=jax.ShapeDtypeStruct((k, n_out), BF16),
        compiler_params=_cp(("parallel",)), name="repack_w_in")(w_in_all)


def _rope_body(x_ref, cos_ref, sin_ref, q_ref, k_ref):
    cos = cos_ref[...]
    sin = sin_ref[...]
    lane = lax.broadcasted_iota(jnp.int32, cos.shape, 1)
    first = (lane % HEAD_DIM) < (HEAD_DIM // 2)
    for j in range((A_WIDTH + A_KV_WIDTH) // LANES):
        x = x_ref[:, j * LANES:(j + 1) * LANES]
        partner = jnp.where(first, pltpu.roll(x, LANES - HEAD_DIM // 2, 1), pltpu.roll(x, HEAD_DIM // 2, 1))
        y = x * cos + partner * sin
        if j < A_WIDTH // LANES:
            q_ref[:, j * LANES:(j + 1) * LANES] = y
        else:
            k_ref[...] = y


def _rope(qkv, cos, sin):
    m, w = qkv.shape
    tm = _pick(m, (256, 128, 16, 8))
    return pl.pallas_call(
        _rope_body, grid=(m // tm,),
        in_specs=[pl.BlockSpec((tm, w), lambda i: (i, 0)), pl.BlockSpec((tm, LANES), lambda i: (i, 0)),
                  pl.BlockSpec((tm, LANES), lambda i: (i, 0))],
        out_specs=[pl.BlockSpec((tm, A_WIDTH), lambda i: (i, 0)), pl.BlockSpec((tm, LANES), lambda i: (i, 0))],
        out_shape=[jax.ShapeDtypeStruct((m, A_WIDTH), F32), jax.ShapeDtypeStruct((m, LANES), F32)],
        compiler_params=_cp(("parallel",)), name="rope")(qkv, cos, sin)


def _attend(q, k_all, v_all, sink_ref, valid, o_ref):
    rows = q.shape[0]
    grp = A_HEADS // A_KV_HEADS
    outs = []
    for h in range(A_KV_HEADS):
        kh = k_all[:, h * HEAD_DIM:(h + 1) * HEAD_DIM]
        vh = v_all[:, h * HEAD_DIM:(h + 1) * HEAD_DIM]
        qs = jnp.concatenate(
            [q[:, (h * grp + g) * HEAD_DIM:(h * grp + g + 1) * HEAD_DIM] for g in range(grp)], axis=0).astype(BF16)
        s = lax.dot_general(qs, kh, (((1,), (1,)), ((), ())), preferred_element_type=F32) * ATTN_SCALE
        if valid is not None:
            s = jnp.where(valid, s, -jnp.inf)
        sink = jnp.concatenate([jnp.full((rows, 1), sink_ref[h * grp + g], F32) for g in range(grp)], axis=0)
        m = jnp.maximum(jnp.max(s, axis=-1, keepdims=True), sink)
        p = jnp.exp(s - m)
        den = jnp.sum(p, axis=-1, keepdims=True) + jnp.exp(sink - m)
        o = jnp.dot((p / den).astype(BF16), vh, preferred_element_type=F32)
        outs += [o[g * rows:(g + 1) * rows] for g in range(grp)]
    o_ref[...] = jnp.concatenate(outs, axis=1).astype(o_ref.dtype)


def _attn_prompt_body(sink_ref, q_ref, k0_ref, k1_ref, k2_ref, v0_ref, v1_ref, v2_ref, o_ref):
    i = pl.program_id(0)
    k_all = jnp.concatenate([k0_ref[...], k1_ref[...], k2_ref[...]], axis=0).astype(BF16)
    v_all = jnp.concatenate([v0_ref[...], v1_ref[...], v2_ref[...]], axis=0).astype(BF16)
    col = lax.broadcasted_iota(jnp.int32, (1, 3 * CHUNK), 1)
    valid = (col >= 2 * CHUNK) | ((col >= CHUNK) & (i >= 1)) | (i >= 2)
    _attend(q_ref[...], k_all, v_all, sink_ref, valid, o_ref)


def _attn_prompt(q_rot, k_rot, qkv, sinks, t):
    nc = t // CHUNK
    vcol = OFF_K // LANES
    kv = lambda d: (lambda i: (jnp.maximum(i - d, 0), 0))
    vv = lambda d: (lambda i: (jnp.maximum(i - d, 0), vcol))
    return pl.pallas_call(
        _attn_prompt_body, grid=(nc,),
        in_specs=[pl.BlockSpec(memory_space=pltpu.SMEM),
                  pl.BlockSpec((CHUNK, A_WIDTH), lambda i: (i, 0)),
                  pl.BlockSpec((CHUNK, LANES), kv(2)), pl.BlockSpec((CHUNK, LANES), kv(1)),
                  pl.BlockSpec((CHUNK, LANES), kv(0)),
                  pl.BlockSpec((CHUNK, LANES), vv(2)), pl.BlockSpec((CHUNK, LANES), vv(1)),
                  pl.BlockSpec((CHUNK, LANES), vv(0))],
        out_specs=pl.BlockSpec((CHUNK, A_WIDTH), lambda i: (i, 0)),
        out_shape=jax.ShapeDtypeStruct((t, A_WIDTH), BF16),
        compiler_params=_cp(("parallel",)), name="attn_prompt")(sinks, q_rot, k_rot, k_rot, k_rot, qkv, qkv, qkv)


def _attn_sample_body(sink_ref, q_ref, kn_ref, vn_ref, ck_ref, cv_ref, o_ref, nk_ref, nv_ref):
    s = q_ref.shape[0]
    kn, vn, ck, cv = kn_ref[...], vn_ref[...], ck_ref[0], cv_ref[0]
    k_all = jnp.concatenate([ck, kn], axis=0)
    v_all = jnp.concatenate([cv, vn], axis=0)
    _attend(q_ref[...], k_all.astype(BF16), v_all.astype(BF16), sink_ref, None, o_ref)
    nk_ref[0] = k_all[s:]
    nv_ref[0] = v_all[s:]


def _attn_sample(q_rot, k_rot, qkv, cache_k, cache_v, sinks, row0, nb, s):
    b0 = row0 // s
    vcol = OFF_K // LANES
    return pl.pallas_call(
        _attn_sample_body, grid=(nb,),
        in_specs=[pl.BlockSpec(memory_space=pltpu.SMEM),
                  pl.BlockSpec((s, A_WIDTH), lambda b: (b0 + b, 0)),
                  pl.BlockSpec((s, LANES), lambda b: (b0 + b, 0)),
                  pl.BlockSpec((s, LANES), lambda b: (b0 + b, vcol)),
                  pl.BlockSpec((1, WINDOW, LANES), lambda b: (b, 0, 0)),
                  pl.BlockSpec((1, WINDOW, LANES), lambda b: (b, 0, 0))],
        out_specs=[pl.BlockSpec((s, A_WIDTH), lambda b: (b, 0)),
                   pl.BlockSpec((1, WINDOW, LANES), lambda b: (b, 0, 0)),
                   pl.BlockSpec((1, WINDOW, LANES), lambda b: (b, 0, 0))],
        out_shape=[jax.ShapeDtypeStruct((nb * s, A_WIDTH), BF16),
                   jax.ShapeDtypeStruct((nb, WINDOW, LANES), F32),
                   jax.ShapeDtypeStruct((nb, WINDOW, LANES), F32)],
        compiler_params=_cp(("parallel",)), name="attn_sample")(sinks, q_rot, k_rot, qkv, cache_k, cache_v)


def _rw_prep_body(f_ref, fp_ref, mu_ref, w0_ref, a0_ref, kk_ref, ka_ref, w2_ref, a2_ref, g2_ref,
                  r_o, k_o, v_o, kk_o, a_o, lw_o, g_o, *, rolled):
    f = f_ref[...]
    if rolled:
        first = lax.broadcasted_iota(jnp.int32, f.shape, 0) == 0
        f_prev = jnp.where(first, fp_ref[0], pltpu.roll(f, 1, 0))
    else:
        f_prev = fp_ref[...]
    fs = f + (f_prev - f) * mu_ref[...]
    w = RW_WIDTH
    r, k, v = fs[:, 0:w], fs[:, w:2 * w], fs[:, 2 * w:3 * w]
    x3 = fs[:, 3 * w:]
    lw = jnp.dot(jnp.tanh(x3).astype(BF16), w2_ref[...], preferred_element_type=F32)
    la = jnp.dot(x3.astype(BF16), a2_ref[...], preferred_element_type=F32)
    g = jnp.dot(jax.nn.sigmoid(x3).astype(BF16), g2_ref[...], preferred_element_type=F32)
    z = -(w0_ref[...] + lw)
    w_log = -(jnp.maximum(z, 0.0) + jnp.log1p(jnp.exp(-jnp.abs(z)))) - 0.5
    a = jax.nn.sigmoid(a0_ref[...] + la)
    r_o[...] = r
    v_o[...] = v
    kk_o[...] = k * kk_ref[...]
    k_o[...] = k * (1.0 + (a - 1.0) * ka_ref[...])
    a_o[...] = a
    lw_o[...] = -jnp.exp(w_log)
    g_o[...] = g


def _rw_prep(f, prev, params, row0, m, tm, rolled):
    i0 = row0 // tm
    row = lambda width: pl.BlockSpec((tm, width), lambda i: (i, 0))
    par = lambda shape: pl.BlockSpec(shape, lambda i: (0, 0))
    prev_spec = pl.BlockSpec((1, 1, RW_PAD), lambda i: (i, 0, 0)) if rolled else row(RW_PAD)
    return pl.pallas_call(
        functools.partial(_rw_prep_body, rolled=rolled), grid=(m // tm,),
        in_specs=[pl.BlockSpec((tm, RW_PAD), lambda i: (i0 + i, 0)), prev_spec, par((1, RW_PAD))]
                 + [par((1, RW_WIDTH))] * 4 + [par((RW_LORA_PAD, RW_WIDTH))] * 3,
        out_specs=[row(RW_WIDTH)] * 7,
        out_shape=[jax.ShapeDtypeStruct((m, RW_WIDTH), F32)] * 7,
        compiler_params=_cp(("parallel",)), name="rwkv_prep")(f, prev, *params)


def _split(x):
    hi = x.astype(BF16)
    return hi, (x - hi.astype(F32)).astype(BF16)


_NN = ((1,), (0,))
_NT = ((1,), (1,))
_TN = ((0,), (0,))


def _dot3(a, b, dims):
    (ah, al), (bh, bl) = a, b
    f = lambda x, y: lax.dot_general(x, y, (dims, ((), ())), preferred_element_type=F32)
    return f(ah, bh) + (f(ah, bl) + f(al, bh))


def _rw_scan_body(r_ref, k_ref, v_ref, kk_ref, a_ref, lw_ref, g_ref, rk_ref, lnw_ref, lnb_ref, s0_ref,
                  o_ref, st_ref, s_scr, *, heads):
    c = pl.program_id(2)
    cs = r_ref.shape[0]
    n = RW_HEAD

    @pl.when(c == 0)
    def _():
        s_scr[...] = s0_ref[0]

    row = lax.broadcasted_iota(jnp.int32, (cs, cs), 0)
    col = lax.broadcasted_iota(jnp.int32, (cs, cs), 1)
    incl = row >= col
    strict = row > col
    ones_tril = incl.astype(BF16)
    steps = cs.bit_length() - 1
    hs = range(heads)
    sls = [slice(h * n, (h + 1) * n) for h in hs]
    r, k, v, kk, a, lw = ([ref[:, sl] for sl in sls] for ref in (r_ref, k_ref, v_ref, kk_ref, a_ref, lw_ref))
    s0 = [s_scr[h] for h in hs]
    kkn = [kk[h] / jnp.maximum(jnp.sqrt(jnp.sum(kk[h] * kk[h], axis=-1, keepdims=True)), 1e-12) for h in hs]
    lw_hi = [lw[h].astype(BF16) for h in hs]
    lw_mid = [(lw[h] - lw_hi[h].astype(F32)).astype(BF16) for h in hs]
    lw_lo = [(lw[h] - lw_hi[h].astype(F32) - lw_mid[h].astype(F32)).astype(BF16) for h in hs]
    tri = lambda part: jnp.dot(ones_tril, part, preferred_element_type=F32)
    cum = [tri(lw_lo[h]) + tri(lw_mid[h]) + tri(lw_hi[h]) for h in hs]
    cum_end = [cum[h][cs - 1:cs, :] for h in hs]
    e_neg = [jnp.exp(-cum[h]) for h in hs]
    e_end = [jnp.exp(cum_end[h] - cum[h]) for h in hs]
    a_t = [-kkn[h] * jnp.exp(cum[h] - lw[h]) for h in hs]
    r_t = [r[h] * jnp.exp(cum[h]) for h in hs]
    ar = [_split(jnp.concatenate([a_t[h], r_t[h]], axis=0)) for h in hs]
    b_s = [_split(kkn[h] * a[h] * e_neg[h]) for h in hs]
    k_s = [_split(k[h] * e_neg[h]) for h in hs]
    v_s = [_split(v[h]) for h in hs]
    s0_s = [_split(s0[h]) for h in hs]
    ab_rb = [_dot3(ar[h], b_s[h], _NT) for h in hs]
    ak_rk = [_dot3(ar[h], k_s[h], _NT) for h in hs]
    l_ak = [_split(jnp.where(strict, ak_rk[h][:cs], 0.0)) for h in hs]
    mv = [_dot3(l_ak[h], v_s[h], _NN) for h in hs]
    x = [jnp.concatenate([a_t[h], mv[h]], axis=1) for h in hs]
    lp = [jnp.where(strict, ab_rb[h][:cs], 0.0) for h in hs]
    for i in range(steps):
        lp_s = [_split(lp[h]) for h in hs]
        x = [x[h] + _dot3(lp_s[h], _split(x[h]), _NN) for h in hs]
        if i < steps - 1:
            lp = [_dot3(lp_s[h], lp_s[h], _NN) for h in hs]
    u = [_dot3(_split(x[h][:, :n]), s0_s[h], _NT) + x[h][:, n:] for h in hs]
    l_rb = [_split(jnp.where(incl, ab_rb[h][cs:], 0.0)) for h in hs]
    l_rk = [_split(jnp.where(incl, ak_rk[h][cs:], 0.0)) for h in hs]
    y = [_dot3((ar[h][0][cs:], ar[h][1][cs:]), s0_s[h], _NT) + _dot3(l_rk[h], v_s[h], _NN) for h in hs]
    y = [y[h] + _dot3(l_rb[h], _split(u[h]), _NN) for h in hs]
    uv = [_split(jnp.concatenate([u[h], v[h]], axis=0)) for h in hs]
    bk = [_split(jnp.concatenate([kkn[h] * a[h] * e_end[h], k[h] * e_end[h]], axis=0)) for h in hs]
    for h in hs:
        s_scr[h] = s0[h] * jnp.exp(cum_end[h]) + _dot3(uv[h], bk[h], _TN)
    outs = []
    for h in hs:
        mean = jnp.mean(y[h], axis=-1, keepdims=True)
        var = jnp.mean(jnp.square(y[h] - mean), axis=-1, keepdims=True)
        yn = (y[h] - mean) * lax.rsqrt(var + RW_LN_EPS) * lnw_ref[:, sls[h]] + lnb_ref[:, sls[h]]
        bonus = jnp.sum(r[h] * k[h] * rk_ref[:, sls[h]], axis=-1, keepdims=True) * v[h]
        outs.append((yn + bonus) * g_ref[:, sls[h]])
    o_ref[...] = jnp.concatenate(outs, axis=1).astype(o_ref.dtype)

    @pl.when(c == pl.num_programs(2) - 1)
    def _():
        st_ref[0] = s_scr[...]


def _rw_scan(prep, r_k, ln_w, ln_b, state0, row0, nb, t, cs, heads=24):
    nc = t // cs
    blk0 = row0 // cs
    width = heads * RW_HEAD
    seq = pl.BlockSpec((cs, width), lambda p, b, c: (blk0 + b * nc + c, p))
    par = pl.BlockSpec((1, width), lambda p, b, c: (0, p))
    st = pl.BlockSpec((1, heads, RW_HEAD, RW_HEAD), lambda p, b, c: (b, p, 0, 0))
    return pl.pallas_call(
        functools.partial(_rw_scan_body, heads=heads), grid=(RW_HEADS // heads, nb, nc),
        in_specs=[seq] * 7 + [par] * 3 + [st],
        out_specs=[pl.BlockSpec((cs, width), lambda p, b, c: (b * nc + c, p)), st],
        out_shape=[jax.ShapeDtypeStruct((nb * t, RW_WIDTH), BF16),
                   jax.ShapeDtypeStruct((nb, RW_HEADS, RW_HEAD, RW_HEAD), F32)],
        scratch_shapes=[pltpu.VMEM((heads, RW_HEAD, RW_HEAD), F32)],
        compiler_params=_cp(("parallel", "parallel", "arbitrary")), name=f"rwkv_scan_{cs}")(
            *prep, r_k, ln_w, ln_b, state0)


def _s5_body(u_ref, bm_ref, cm_ref, lp_ref, d_ref, x0_ref, y_ref, xt_ref, x_scr, cin_scr, carry_scr, *, exact):
    i = pl.program_id(2)
    tt = u_ref.shape[0]
    nb = tt // SUBLANES
    w = S5_TILE_ST

    @pl.when(i == 0)
    def _():
        carry_scr[...] = x0_ref[0, 0]

    u = u_ref[...]
    if exact:
        bu = jnp.dot(u, bm_ref[0], precision=HIGHEST, preferred_element_type=F32)
    else:
        bu = jnp.dot(u.astype(BF16), bm_ref[0].astype(BF16), preferred_element_type=F32)
    nct = w // LANES
    for ct in range(2 * nct):
        x_scr[ct] = bu[:, ct * LANES:(ct + 1) * LANES]

    def lam_pow(t, ct):
        return (lp_ref[0, t:t + 1, ct * LANES:(ct + 1) * LANES],
                lp_ref[0, t:t + 1, w + ct * LANES:w + (ct + 1) * LANES])

    def cmul_add(xr, xi, cr, ci, sr, si):
        return xr + cr * sr - ci * si, xi + cr * si + ci * sr

    def rows(ct, t):
        return x_scr.at[ct, pl.ds(t, nb, stride=SUBLANES), :]

    for ct in range(nct):
        lam_r, lam_i = lam_pow(0, ct)
        for t in range(1, SUBLANES):
            nr, ni = cmul_add(rows(ct, t)[...], rows(nct + ct, t)[...], lam_r, lam_i,
                              rows(ct, t - 1)[...], rows(nct + ct, t - 1)[...])
            rows(ct, t)[...] = nr
            rows(nct + ct, t)[...] = ni

    lam8 = jnp.concatenate([lp_ref[0, SUBLANES - 1:SUBLANES, :w], lp_ref[0, SUBLANES - 1:SUBLANES, w:]], axis=0)

    def block_carry(b, carry):
        last = b * SUBLANES + SUBLANES - 1
        for ct in range(2 * nct):
            half, lt = divmod(ct, nct)
            cin_scr[ct, pl.ds(b, 1), :] = carry[half:half + 1, lt * LANES:(lt + 1) * LANES]
        end_r = jnp.concatenate([x_scr[ct, pl.ds(last, 1), :] for ct in range(nct)], axis=1)
        end_i = jnp.concatenate([x_scr[nct + ct, pl.ds(last, 1), :] for ct in range(nct)], axis=1)
        nr, ni = cmul_add(end_r, end_i, lam8[0:1], lam8[1:2], carry[0:1], carry[1:2])
        return jnp.concatenate([nr, ni], axis=0)

    carry0 = jnp.concatenate([carry_scr[:, :w], carry_scr[:, w:]], axis=0)
    carry = lax.fori_loop(0, nb, block_carry, carry0)
    carry_scr[...] = jnp.concatenate([carry[0:1], carry[1:2]], axis=1)
    for ct in range(nct):
        cin_r, cin_i = cin_scr[ct], cin_scr[nct + ct]
        for t in range(SUBLANES):
            pr, pi_ = lam_pow(t, ct)
            nr, ni = cmul_add(rows(ct, t)[...], rows(nct + ct, t)[...], pr, pi_, cin_r, cin_i)
            rows(ct, t)[...] = nr
            rows(nct + ct, t)[...] = ni

    xs = jnp.concatenate([x_scr[ct] for ct in range(2 * nct)], axis=1)
    if exact:
        y = jnp.dot(xs, cm_ref[0], precision=HIGHEST, preferred_element_type=F32)
    else:
        y = jnp.dot(xs.astype(BF16), cm_ref[0].astype(BF16), preferred_element_type=F32)
    y_ref[...] = y + d_ref[...] * u

    @pl.when(i == pl.num_programs(2) - 1)
    def _():
        xt_ref[0, 0] = carry_scr[...]


def _s5_scan(u, bm, cm, lp, d, x0, row0, nb_seq, t, tt, exact):
    nt = t // tt
    i0 = row0 // tt
    w2 = 2 * S5_TILE_ST
    return pl.pallas_call(
        functools.partial(_s5_body, exact=exact), grid=(S5_TILES, nb_seq, nt),
        in_specs=[pl.BlockSpec((tt, S5_TILE_IN), lambda j, b, i: (i0 + b * nt + i, j)),
                  pl.BlockSpec((1, S5_TILE_IN, w2), lambda j, b, i: (j, 0, 0)),
                  pl.BlockSpec((1, w2, S5_TILE_IN), lambda j, b, i: (j, 0, 0)),
                  pl.BlockSpec((1, SUBLANES, w2), lambda j, b, i: (j, 0, 0)),
                  pl.BlockSpec((1, S5_TILE_IN), lambda j, b, i: (0, j)),
                  pl.BlockSpec((1, 1, 1, w2), lambda j, b, i: (b, j, 0, 0))],
        out_specs=[pl.BlockSpec((tt, S5_TILE_IN), lambda j, b, i: (b * nt + i, j)),
                   pl.BlockSpec((1, 1, 1, w2), lambda j, b, i: (b, j, 0, 0))],
        out_shape=[jax.ShapeDtypeStruct((nb_seq * t, S5_WIDTH), F32),
                   jax.ShapeDtypeStruct((nb_seq, S5_TILES, 1, w2), F32)],
        scratch_shapes=[pltpu.VMEM((w2 // LANES, tt, LANES), F32), pltpu.VMEM((w2 // LANES, tt // SUBLANES, LANES), F32),
                        pltpu.VMEM((1, w2), F32)],
        compiler_params=_cp(("parallel", "parallel", "arbitrary")), name=f"s5_scan_{tt}")(u, bm, cm, lp, d, x0)


def _gelu_tanh(y):
    return 0.5 * y * (1.0 + jnp.tanh(math.sqrt(2.0 / math.pi) * (y + 0.044715 * (y * y * y))))


def _s5_glu_body(y_ref, w_ref, b_ref, o_ref, *, tn):
    j = pl.program_id(1)
    z = _gelu_tanh(y_ref[...])
    gate = jax.nn.sigmoid(jnp.dot(z.astype(BF16), w_ref[...], preferred_element_type=F32) + b_ref[...])
    zj = _gelu_tanh(y_ref[:, pl.ds(pl.multiple_of(j * tn, tn), tn)])
    o_ref[...] = (zj * gate).astype(o_ref.dtype)


def _s5_glu(y, w, b, tn=512):
    m, d = y.shape
    tm = _pick(m, (256, 128, 16, 8))
    return pl.pallas_call(
        functools.partial(_s5_glu_body, tn=tn), grid=(m // tm, d // tn),
        in_specs=[pl.BlockSpec((tm, d), lambda i, j: (i, 0)), pl.BlockSpec((d, tn), lambda i, j: (0, j)),
                  pl.BlockSpec((1, tn), lambda i, j: (0, j))],
        out_specs=pl.BlockSpec((tm, tn), lambda i, j: (i, j)),
        out_shape=jax.ShapeDtypeStruct((m, d), BF16),
        compiler_params=_cp(("parallel", "parallel")), name="s5_glu")(y, w, b)


def _merge_body(xn_ref, oa_ref, ob_ref, oc_ref, ga_ref, gb_ref, gc_ref, wa_ref, wb_ref, wc_ref, o_ref):
    xn = xn_ref[...]
    gate = lambda g_ref: jax.nn.sigmoid(jnp.dot(xn, g_ref[...], preferred_element_type=F32))
    acc = gate(ga_ref) * jnp.dot(oa_ref[...], wa_ref[...], preferred_element_type=F32)
    acc = acc + gate(gb_ref) * jnp.dot(ob_ref[...], wb_ref[...], preferred_element_type=F32)
    acc = acc + gate(gc_ref) * jnp.dot(oc_ref[...], wc_ref[...], preferred_element_type=F32)
    o_ref[...] = acc.astype(o_ref.dtype)


def _merge(xn, w_pk, gate_col0, o_a, o_b, o_c, w_a, w_b, w_c, tn=256):
    m, d = xn.shape
    tm = _row_tile(m)
    nj = D_MODEL // tn
    j0 = gate_col0 // tn
    full = lambda width: pl.BlockSpec((tm, width), lambda i, j: (i, 0))
    gate = lambda which: pl.BlockSpec((d, tn), lambda i, j: (0, j0 + which * nj + j))
    wcol = lambda rows: pl.BlockSpec((rows, tn), lambda i, j: (0, j))
    return pl.pallas_call(
        _merge_body, grid=(m // tm, nj),
        in_specs=[full(d), full(A_WIDTH), full(RW_WIDTH), full(S5_WIDTH), gate(0), gate(1), gate(2),
                  wcol(A_WIDTH), wcol(RW_WIDTH), wcol(S5_WIDTH)],
        out_specs=pl.BlockSpec((tm, tn), lambda i, j: (i, j)),
        out_shape=jax.ShapeDtypeStruct((m, D_MODEL), BF16),
        compiler_params=_cp(("parallel", "parallel")), name="merge")(
            xn, o_a, o_b, o_c, w_pk, w_pk, w_pk, w_a, w_b, w_c)


def _dispatch_body(nu_ref, tok_ref, x_hbm, o_ref, buf, sem, *, tg, per_tile):
    i = pl.program_id(0)
    b = i % 2
    live = lambda s: s // per_tile < nu_ref[0]

    def row_copy(s, bb, r):
        return pltpu.make_async_copy(x_hbm.at[pl.ds(tok_ref[s * tg + r], 1)], buf.at[bb, pl.ds(r, 1)], sem.at[bb])

    def start_gather(s, bb):
        def body(r, c):
            row_copy(s, bb, r).start()
            return c
        lax.fori_loop(0, tg, body, 0)

    @pl.when(jnp.logical_and(i == 0, live(0)))
    def _():
        start_gather(0, 0)

    @pl.when(jnp.logical_and(i + 1 < pl.num_programs(0), live(i + 1)))
    def _():
        start_gather(i + 1, 1 - b)

    @pl.when(live(i))
    def _():
        def body(r, c):
            row_copy(i, b, r).wait()
            return c
        lax.fori_loop(0, tg, body, 0)
        o_ref[...] = buf[b].astype(o_ref.dtype)

    @pl.when(jnp.logical_not(live(i)))
    def _():
        o_ref[...] = jnp.zeros_like(o_ref)


def _dispatch(x, slot_tok, n_used):
    d = x.shape[1]
    tg = 128
    per_tile = MOE_TM // tg
    subs = slot_tok.shape[0] // tg
    grid_spec = pltpu.PrefetchScalarGridSpec(
        num_scalar_prefetch=2, grid=(subs,),
        in_specs=[pl.BlockSpec(memory_space=pl.ANY)],
        out_specs=pl.BlockSpec((tg, d), lambda i, nu, tok: (i, 0)),
        scratch_shapes=[pltpu.VMEM((2, tg, d), F32), pltpu.SemaphoreType.DMA((2,))])
    return pl.pallas_call(
        functools.partial(_dispatch_body, tg=tg, per_tile=per_tile), grid_spec=grid_spec,
        out_shape=jax.ShapeDtypeStruct((slot_tok.shape[0], d), BF16),
        compiler_params=_cp(("arbitrary",)), name="moe_dispatch")(n_used, slot_tok, x)


def _expert_body(te_ref, nu_ref, x_ref, wg_ref, wu_ref, wd_ref, o_ref, hg_acc, hu_acc, hb):
    i, j = pl.program_id(0), pl.program_id(1)
    used = i < nu_ref[0]
    down = j >= MOE_KSTEPS

    @pl.when(jnp.logical_and(used, jnp.logical_not(down)))
    def _():
        x = x_ref[:, pl.ds(pl.multiple_of(j * MOE_TK, MOE_TK), MOE_TK)]
        pg = jnp.dot(x, wg_ref[...].astype(BF16), preferred_element_type=F32)
        pu = jnp.dot(x, wu_ref[...].astype(BF16), preferred_element_type=F32)

        @pl.when(j == 0)
        def _():
            hg_acc[...] = pg
            hu_acc[...] = pu

        @pl.when(j > 0)
        def _():
            hg_acc[...] += pg
            hu_acc[...] += pu

    @pl.when(jnp.logical_and(used, j == MOE_KSTEPS))
    def _():
        hg = hg_acc[...]
        hb[...] = (hg * jax.nn.sigmoid(hg) * hu_acc[...]).astype(BF16)

    @pl.when(jnp.logical_and(used, down))
    def _():
        o_ref[...] = jnp.dot(hb[...], wd_ref[...].astype(BF16), preferred_element_type=F32)

    @pl.when(jnp.logical_and(jnp.logical_not(used), down))
    def _():
        o_ref[...] = jnp.zeros_like(o_ref)


def _experts(xg, tile_expert, n_used, layer, w_gate, w_up, w_down):
    rows, d = xg.shape
    n_tiles = rows // MOE_TM
    n_down = d // MOE_TN

    def wmap(i, j, te, nu):
        live = i < nu[0]
        return layer, te[jnp.minimum(i, nu[0] - 1)], jnp.where(live, jnp.minimum(j, MOE_KSTEPS - 1), MOE_KSTEPS - 1), 0

    def wdmap(i, j, te, nu):
        live = i < nu[0]
        return layer, te[jnp.minimum(i, nu[0] - 1)], 0, jnp.where(live, jnp.maximum(j - MOE_KSTEPS, 0), n_down - 1)

    grid_spec = pltpu.PrefetchScalarGridSpec(
        num_scalar_prefetch=2, grid=(n_tiles, MOE_KSTEPS + n_down),
        in_specs=[pl.BlockSpec((MOE_TM, d), lambda i, j, te, nu: (i, 0)),
                  pl.BlockSpec((None, None, MOE_TK, D_EXPERT), wmap), pl.BlockSpec((None, None, MOE_TK, D_EXPERT), wmap),
                  pl.BlockSpec((None, None, D_EXPERT, MOE_TN), wdmap)],
        out_specs=pl.BlockSpec((MOE_TM, MOE_TN), lambda i, j, te, nu: (i, jnp.maximum(j - MOE_KSTEPS, 0))),
        scratch_shapes=[pltpu.VMEM((MOE_TM, D_EXPERT), F32), pltpu.VMEM((MOE_TM, D_EXPERT), F32),
                        pltpu.VMEM((MOE_TM, D_EXPERT), BF16)])
    return pl.pallas_call(
        _expert_body, grid_spec=grid_spec, out_shape=jax.ShapeDtypeStruct((rows, d), F32),
        compiler_params=_cp(("parallel", "arbitrary"), 56), name="experts")(
            tile_expert, n_used, xg, w_gate, w_up, w_down)


def _combine_body(slot_ref, x_ref, g_ref, out_hbm, o_ref, buf, sem, *, tc):
    i = pl.program_id(0)
    b = i % 2

    def row_copy(tile, bb, r):
        return pltpu.make_async_copy(out_hbm.at[pl.ds(slot_ref[tile * 2 * tc + r], 1)], buf.at[bb, pl.ds(r, 1)],
                                     sem.at[bb])

    def start_gather(tile, bb):
        def body(r, c):
            row_copy(tile, bb, r).start()
            return c
        lax.fori_loop(0, 2 * tc, body, 0)

    @pl.when(i == 0)
    def _():
        start_gather(0, 0)

    @pl.when(i + 1 < pl.num_programs(0))
    def _():
        start_gather(i + 1, 1 - b)

    def wait_body(r, c):
        row_copy(i, b, r).wait()
        return c
    lax.fori_loop(0, 2 * tc, wait_body, 0)
    rows = buf[b]
    g = g_ref[...]
    o_ref[...] = x_ref[...] + (g[:, 0:1] * rows[:tc] + g[:, 1:2] * rows[tc:])


def _combine(x, gate, slot, out):
    n, d = x.shape
    tc = _pick(n, (128, 96, 64, 32, 16, 8))
    tiles = n // tc
    slot_flat = slot.reshape(tiles, tc, TOP_K).transpose(0, 2, 1).reshape(-1)
    grid_spec = pltpu.PrefetchScalarGridSpec(
        num_scalar_prefetch=1, grid=(tiles,),
        in_specs=[pl.BlockSpec((tc, d), lambda i, s: (i, 0)), pl.BlockSpec((tc, TOP_K), lambda i, s: (i, 0)),
                  pl.BlockSpec(memory_space=pl.ANY)],
        out_specs=pl.BlockSpec((tc, d), lambda i, s: (i, 0)),
        scratch_shapes=[pltpu.VMEM((2, TOP_K * tc, d), F32), pltpu.SemaphoreType.DMA((2,))])
    return pl.pallas_call(
        functools.partial(_combine_body, tc=tc), grid_spec=grid_spec, out_shape=jax.ShapeDtypeStruct((n, d), F32),
        compiler_params=_cp(("arbitrary",)), name="moe_combine")(slot_flat, x, gate, out)


def _moe(x, xn, rw, rg_b, re_b, layer, w_gate, w_up, w_down):
    n, d = xn.shape
    logits = _matmul(xn, rw, tn=LANES, name="router")
    g_logits = logits[:, :N_GROUPS] + rg_b
    grp = jnp.argmax(g_logits, axis=-1).astype(jnp.int32)
    p_grp = jnp.take_along_axis(jax.nn.softmax(g_logits, axis=-1), grp[:, None], axis=-1)
    e_logits = (logits[:, N_GROUPS:N_GROUPS + N_EXPERTS] + re_b).reshape(n, N_GROUPS, EXPERTS_PER_GROUP)
    e_logits = jnp.take_along_axis(e_logits, grp[:, None, None], axis=1)[:, 0]
    top_v, top_i = lax.top_k(e_logits, TOP_K)
    gate = jax.nn.softmax(top_v, axis=-1) * p_grp
    expert_id = (grp[:, None] * EXPERTS_PER_GROUP + top_i).astype(jnp.int32).reshape(-1)

    a = n * TOP_K
    n_tiles = -(-(a + N_EXPERTS * (MOE_TM - 1)) // MOE_TM)
    idx = jnp.arange(a, dtype=jnp.int32)
    order = jnp.argsort(expert_id * a + idx)
    e_sorted = expert_id[order]
    counts = jnp.bincount(expert_id, length=N_EXPERTS).astype(jnp.int32)
    starts = jnp.cumsum(counts) - counts
    padded = (counts + MOE_TM - 1) // MOE_TM * MOE_TM
    pad_end = jnp.cumsum(padded)
    pad_start = pad_end - padded
    slot_sorted = (pad_start[e_sorted] + idx - starts[e_sorted]).astype(jnp.int32)
    slot = jnp.zeros((a,), jnp.int32).at[order].set(slot_sorted)
    slot_tok = jnp.zeros((n_tiles * MOE_TM,), jnp.int32).at[slot].set(idx // TOP_K)
    tile_expert = jnp.minimum(
        jnp.searchsorted(pad_end, jnp.arange(n_tiles, dtype=jnp.int32) * MOE_TM, side='right'),
        N_EXPERTS - 1).astype(jnp.int32)
    n_used = (pad_end[-1:] // MOE_TM).astype(jnp.int32)
    out = _experts(_dispatch(xn, slot_tok, n_used), tile_expert, n_used, layer, w_gate, w_up, w_down)
    return _combine(x, gate, slot.reshape(n, TOP_K), out)


def _s5_params(lam_re, lam_im, b_re, b_im, c_re, c_im, log_step):
    dt = jnp.exp(log_step)[:, None]
    mag = jnp.exp(lam_re * dt)
    lbr, lbi = mag * jnp.cos(lam_im * dt), mag * jnp.sin(lam_im * dt)
    den = lam_re * lam_re + lam_im * lam_im
    qr = ((lbr - 1.0) * lam_re + lbi * lam_im) / den
    qi = (lbi * lam_re - (lbr - 1.0) * lam_im) / den
    bbr = qr[:, :, None] * b_re - qi[:, :, None] * b_im
    bbi = qr[:, :, None] * b_im + qi[:, :, None] * b_re
    eye = jnp.eye(S5_TILE_GROUPS, dtype=F32)

    def blockdiag_in(m):
        m = m.reshape(S5_TILES, S5_TILE_GROUPS, S5_STATE, S5_GROUP)
        return jnp.einsum('jgph,ge->jghep', m, eye).reshape(S5_TILES, S5_TILE_IN, S5_TILE_ST)

    def blockdiag_out(m):
        m = m.reshape(S5_TILES, S5_TILE_GROUPS, S5_GROUP, S5_STATE)
        return jnp.einsum('jghp,ge->jgpeh', m, eye).reshape(S5_TILES, S5_TILE_ST, S5_TILE_IN)

    bm = jnp.concatenate([blockdiag_in(bbr), blockdiag_in(bbi)], axis=2)
    cm = jnp.concatenate([blockdiag_out(c_re), blockdiag_out(-c_im)], axis=1)
    pr, pi_ = [lbr], [lbi]
    for _ in range(SUBLANES - 1):
        pr, pi_ = pr + [pr[-1] * lbr - pi_[-1] * lbi], pi_ + [pr[-1] * lbi + pi_[-1] * lbr]
    tile = lambda p: jnp.stack(p, 0).reshape(SUBLANES, S5_TILES, S5_TILE_ST).transpose(1, 0, 2)
    lp = jnp.concatenate([tile(pr), tile(pi_)], axis=2)
    return bm, cm, lp


def _s5_state_in(re, im):
    b = re.shape[0]
    f = lambda x: x.reshape(b, S5_TILES, 1, S5_TILE_ST)
    return jnp.concatenate([f(re), f(im)], axis=3)


def _s5_state_out(x):
    b = x.shape[0]
    return (x[..., 0, :S5_TILE_ST].reshape(b, S5_GROUPS, S5_STATE), x[..., 0, S5_TILE_ST:].reshape(b, S5_GROUPS, S5_STATE))


def _pad_rows(w, rows, at):
    return jnp.zeros((rows, w.shape[1]), w.dtype).at[at:at + w.shape[0]].set(w)


def _layer(x, tp, nbs, ts, rope_cos, rope_sin, cache_k, cache_v, shift0, rw0, s5re0, s5im0, lw, stacked):
    n = x.shape[0]
    bf = lambda w: w.astype(BF16)
    layer, w_in_all = stacked[0], stacked[1]
    experts = (layer,) + tuple(stacked[2:])
    w_pk = _repack_w_in(w_in_all, layer)
    gate_col0 = OFF_V + RW_PAD + S5_WIDTH
    pk_s5 = OFF_V + RW_PAD
    xn = _rmsnorm(x, lw['norm_mix_g'], BF16)
    qkv = _matmul(xn, w_pk, col0=0, n=OFF_V, tn=256, name="proj_qkv")
    f = _matmul(xn, w_pk, col0=OFF_V, n=RW_PAD, tn=256, name="proj_rwkv")
    u = _matmul(xn, w_pk, col0=pk_s5, n=S5_WIDTH, name="proj_s5")

    q_rot, k_rot = _rope(qkv, rope_cos, rope_sin)
    o_ap = _attn_prompt(q_rot, k_rot, qkv, lw['attn_sinks'], tp)
    ck = cache_k.reshape(nbs, WINDOW, A_KV_WIDTH)
    cv = cache_v.reshape(nbs, WINDOW, A_KV_WIDTH)
    o_as, nk_s, nv_s = _attn_sample(q_rot, k_rot, qkv, ck, cv, lw['attn_sinks'], tp, nbs, ts)
    o_a = jnp.concatenate([o_ap, o_as], axis=0)
    kv_shape = (-1, WINDOW, A_KV_HEADS, HEAD_DIM)
    nk_p = k_rot[tp - WINDOW:tp].reshape(kv_shape)
    nv_p = qkv[tp - WINDOW:tp, OFF_K:OFF_V].reshape(kv_shape)

    ms = nbs * ts
    tm_p = _pick(tp, (256, 128, 64))
    assert tp % ms == 0, "sample rows must start on a tile boundary"
    above = jnp.concatenate([jnp.zeros((1, RW_PAD), F32), f[tm_p - 1:tp - 1:tm_p]], axis=0)[:, None, :]
    f_s = f[tp:].reshape(nbs, ts, RW_PAD)
    sh = jnp.pad(shift0, ((0, 0), (0, RW_PAD - SHIFT_WIDTH)))
    fp_s = jnp.concatenate([sh[:, None, :], f_s[:, :-1]], axis=1).reshape(ms, RW_PAD)
    row = lambda v: v.reshape(1, -1)
    prep_par = (jnp.pad(row(lw['rwkv_mu']), ((0, 0), (0, RW_PAD - SHIFT_WIDTH))), row(lw['rwkv_w0']),
                row(lw['rwkv_a0']), row(lw['rwkv_k_k']), row(lw['rwkv_k_a']),
                bf(_pad_rows(lw['rwkv_w2'], RW_LORA_PAD, 0)), bf(_pad_rows(lw['rwkv_a2'], RW_LORA_PAD, RW_LORA)),
                bf(_pad_rows(lw['rwkv_g2'], RW_LORA_PAD, 2 * RW_LORA)))
    prep_p = _rw_prep(f, above, prep_par, 0, tp, tm_p, True)
    prep_s = _rw_prep(f, fp_s, prep_par, tp, ms, ms, False)
    rw_par = (row(lw['rwkv_r_k']), row(lw['rwkv_ln_w']), row(lw['rwkv_ln_b']))
    zero_rw = jnp.zeros((1, RW_HEADS, RW_HEAD, RW_HEAD), F32)
    o_bp, rw_p = _rw_scan(prep_p, *rw_par, zero_rw, 0, 1, tp, CHUNK)
    o_bs, rw_s = _rw_scan(prep_s, *rw_par, rw0, 0, nbs, ts, ts)
    o_b = jnp.concatenate([o_bp, o_bs], axis=0)
    shift_p = f[tp - 1:tp, :SHIFT_WIDTH]
    shift_s = f_s[:, -1, :SHIFT_WIDTH]

    bm, cm, lp = _s5_params(lw['s5_lambda_re'], lw['s5_lambda_im'], lw['s5_b_re'], lw['s5_b_im'],
                            lw['s5_c_re'], lw['s5_c_im'], lw['s5_log_step'])
    d_row = row(lw['s5_d'])
    zero_s5 = jnp.zeros((1, S5_TILES, 1, 2 * S5_TILE_ST), F32)
    y_p, xs_p = _s5_scan(u, bm, cm, lp, d_row, zero_s5, 0, 1, tp, tm_p, False)
    y_s, xs_s = _s5_scan(u, bm, cm, lp, d_row, _s5_state_in(s5re0, s5im0), tp, nbs, ts, ts, True)
    o_c = _s5_glu(jnp.concatenate([y_p, y_s], axis=0), bf(lw['s5_glu_w']), row(lw['s5_glu_b']))
    s5re_p, s5im_p = _s5_state_out(xs_p)
    s5re_s, s5im_s = _s5_state_out(xs_s)

    merged = _merge(xn, w_pk, gate_col0, o_a, o_b, o_c,
                    bf(lw['w_branch_attn']), bf(lw['w_branch_rwkv']), bf(lw['w_branch_s5']))
    x = _matmul_residual(merged, bf(lw['w_out']), x)

    xn2 = _rmsnorm(x, lw['norm_ffn_g'], F32)
    rw = jnp.concatenate([lw['router_group_w'], lw['router_expert_w'],
                          jnp.zeros((D_MODEL, LANES - N_GROUPS - N_EXPERTS), F32)], axis=1)
    x = _moe(x, xn2, bf(rw), lw['router_group_b'], lw['router_expert_b'], *experts)
    st_p = (nk_p, nv_p, shift_p, rw_p, s5re_p, s5im_p)
    st_s = (nk_s.reshape(kv_shape), nv_s.reshape(kv_shape), shift_s, rw_s, s5re_s, s5im_s)
    return x, st_p, st_s


def _rope_tables(pos):
    half = HEAD_DIM // 2
    inv_freq = ROPE_THETA ** (-jnp.arange(half, dtype=F32) / half)
    ang = pos.astype(F32)[:, None] * inv_freq[None, :]
    cos, sin = jnp.cos(ang), jnp.sin(ang)
    reps = LANES // HEAD_DIM
    return jnp.tile(jnp.concatenate([cos, cos], axis=1), (1, reps)), jnp.tile(jnp.concatenate([-sin, sin], axis=1), (1, reps))


_LAYER_KEYS = ('norm_mix_g', 'w_in', 'attn_sinks', 'rwkv_mu', 'rwkv_w0', 'rwkv_w2', 'rwkv_a0', 'rwkv_a2', 'rwkv_g2',
               'rwkv_k_k', 'rwkv_k_a', 'rwkv_r_k', 'rwkv_ln_w', 'rwkv_ln_b', 's5_lambda_re', 's5_lambda_im',
               's5_b_re', 's5_b_im', 's5_c_re', 's5_c_im', 's5_d', 's5_log_step', 's5_glu_w', 's5_glu_b',
               'w_branch_attn', 'w_branch_rwkv', 'w_branch_s5', 'w_out', 'norm_ffn_g', 'router_group_w',
               'router_group_b', 'router_expert_w', 'router_expert_b', 'expert_w_gate', 'expert_w_up', 'expert_w_down')


def _forward(x_prompt, x_sample, cache_k, cache_v, state_shift, state_rwkv, state_s5_re, state_s5_im,
             layer_weights, norm_final_g):
    bp, tp, d = x_prompt.shape
    assert bp == 1, "the prompt batch is one new stream"
    nbs, ts, _ = x_sample.shape
    depth = cache_k.shape[0]
    x = jnp.concatenate([x_prompt.reshape(tp, d), x_sample.reshape(nbs * ts, d)], axis=0)
    pos = jnp.concatenate([jnp.arange(tp, dtype=jnp.int32),
                           jnp.tile(PAST_LEN + jnp.arange(ts, dtype=jnp.int32), nbs)])
    rope_cos, rope_sin = _rope_tables(pos)
    new_p = [[] for _ in range(6)]
    new_s = [[] for _ in range(6)]
    w_in_bf = layer_weights[_LAYER_KEYS.index('w_in')].astype(BF16)
    for l in range(depth):
        big = ('w_in', 'expert_w_gate', 'expert_w_up', 'expert_w_down')
        lw = {k: v[l] for k, v in zip(_LAYER_KEYS, layer_weights) if k not in big}
        stacked = (l, w_in_bf) + tuple(layer_weights[_LAYER_KEYS.index(k)] for k in big[1:])
        x, st_p, st_s = _layer(x, tp, nbs, ts, rope_cos, rope_sin, cache_k[l], cache_v[l], state_shift[l],
                               state_rwkv[l], state_s5_re[l], state_s5_im[l], lw, stacked)
        for i in range(6):
            new_p[i].append(st_p[i])
            new_s[i].append(st_s[i])
    y = _rmsnorm(x, norm_final_g, F32)
    outs_p = tuple(jnp.stack(t, axis=0) for t in new_p)
    outs_s = tuple(jnp.stack(t, axis=0) for t in new_s)
    return (y[:tp].reshape(1, tp, d), y[tp:].reshape(nbs, ts, d)) + outs_p + outs_s


def kernel(x_prompt, x_sample, cache_k, cache_v, state_shift, state_rwkv, state_s5_re, state_s5_im,
           norm_mix_g, w_in, attn_sinks, rwkv_mu, rwkv_w0, rwkv_w2, rwkv_a0, rwkv_a2, rwkv_g2,
           rwkv_k_k, rwkv_k_a, rwkv_r_k, rwkv_ln_w, rwkv_ln_b, s5_lambda_re, s5_lambda_im,
           s5_b_re, s5_b_im, s5_c_re, s5_c_im, s5_d, s5_log_step, s5_glu_w, s5_glu_b,
           w_branch_attn, w_branch_rwkv, w_branch_s5, w_out, norm_ffn_g, router_group_w,
           router_group_b, router_expert_w, router_expert_b, expert_w_gate, expert_w_up,
           expert_w_down, norm_final_g):
    layer_weights = (norm_mix_g, w_in, attn_sinks, rwkv_mu, rwkv_w0, rwkv_w2, rwkv_a0, rwkv_a2, rwkv_g2,
                     rwkv_k_k, rwkv_k_a, rwkv_r_k, rwkv_ln_w, rwkv_ln_b, s5_lambda_re, s5_lambda_im,
                     s5_b_re, s5_b_im, s5_c_re, s5_c_im, s5_d, s5_log_step, s5_glu_w, s5_glu_b,
                     w_branch_attn, w_branch_rwkv, w_branch_s5, w_out, norm_ffn_g, router_group_w,
                     router_group_b, router_expert_w, router_expert_b, expert_w_gate, expert_w_up, expert_w_down)
    return _forward(x_prompt, x_sample, cache_k, cache_v, state_shift, state_rwkv, state_s5_re, state_s5_im,
                    layer_weights, norm_final_g)
```
